```python
import jax, jax.numpy as jnp
from jax import lax
import numpy as np

D_MODEL = 2048
BATCH = 2
SEQ = 8192
DEPTH = 4
DEC_BATCH = 8
DEC_SEQ = 16
PAST_LEN = 2048

CHUNK = 64
CM_BLOCK = 128
CM_GROUPS = 4
CM_DIM = D_MODEL // 16
GLA_HEADS = 4
GLA_DK = D_MODEL // 32
GLA_DV = D_MODEL // 16
GLA_RANK = 16
GLA_TAU = 16.0
SB_HEADS = 8
SB_DIM = D_MODEL // 16
SB_BLOCK = 128
D_FF = 11 * D_MODEL // 4
CONV_W = 3
EPS = 1e-6

CM_W = CM_GROUPS * CM_DIM
GLA_K_W = GLA_HEADS * GLA_DK
GLA_V_W = GLA_HEADS * GLA_DV
SB_W = SB_HEADS * SB_DIM
MIX_W = CM_W + GLA_V_W + SB_W
IN_SIZES = (CM_W, CM_W, GLA_K_W, GLA_K_W, GLA_V_W, GLA_V_W, GLA_RANK, SB_W, SB_W, SB_W)
IN_W = CM_W * 2 + GLA_K_W * 2 + GLA_V_W * 2 + GLA_RANK + SB_W * 3

kernel_name = "hybrid_streaming_encoder_step"


def _rms_norm(x, g):
    xf = x.astype(jnp.float32)
    y = xf * lax.rsqrt(jnp.mean(xf * xf, axis=-1, keepdims=True) + EPS)
    return (y * g).astype(x.dtype)


def _split_in(proj):
    offsets = np.cumsum(IN_SIZES)[:-1].tolist()
    return jnp.split(proj, offsets, axis=-1)


def _chunk_mlp(u, v, ws, b, prompt):
    u = jax.nn.gelu(u)
    vf = jax.nn.gelu(v).astype(jnp.float32)
    mu = jnp.mean(vf, axis=-1, keepdims=True)
    var = jnp.mean(jnp.square(vf - mu), axis=-1, keepdims=True)
    vn = ((vf - mu) * lax.rsqrt(var + EPS)).astype(v.dtype)
    pos = jnp.arange(CM_BLOCK)
    mask = (pos[None, :] // CHUNK) <= (pos[:, None] // CHUNK)
    wsm = jnp.where(mask, ws, 0.0)
    bsz, t, g, c = vn.shape
    if prompt:
        vb = vn.reshape(bsz, t // CM_BLOCK, CM_BLOCK, g, c)
        s = jnp.einsum('gij,bnjgc->bnigc', wsm, vb) + b.T[:, :, None]
        s = s.reshape(bsz, t, g, c)
    else:
        s = jnp.einsum('gij,bjgc->bigc', wsm[:, :t, :t], vn) + b[:, :t].T[:, :, None]
    return u * s, vn


def _gla(q, k, v, log_a, s0, chunk):
    bsz, t, h, _ = q.shape
    dv = v.shape[-1]
    n = t // chunk

    def to_chunks(a):
        return jnp.moveaxis(a.astype(jnp.float32).reshape(bsz, n, chunk, h, a.shape[-1]), 1, 0)

    tri = jnp.tril(jnp.ones((chunk, chunk), dtype=bool))

    def step(s, inp):
        qc, kc, vc, ac = inp
        cum = jnp.cumsum(ac, axis=1)
        qe = qc * jnp.exp(cum)
        ke = kc * jnp.exp(-cum)
        att = jnp.where(tri, jnp.einsum('bthd,bshd->bhts', qe, ke), 0.0)
        o = jnp.einsum('bhts,bshv->bthv', att, vc) + jnp.einsum('bthd,bhdv->bthv', qe, s)
        last = cum[:, -1]
        kd = kc * jnp.exp(last[:, None] - cum)
        s = jnp.exp(last)[..., None] * s + jnp.einsum('bshd,bshv->bhdv', kd, vc)
        return s, o

    s, o = lax.scan(step, s0.astype(jnp.float32),
                    (to_chunks(q), to_chunks(k), to_chunks(v), to_chunks(log_a)))
    o = jnp.moveaxis(o, 0, 1).reshape(bsz, t, h, dv)
    return o.astype(v.dtype), s.astype(v.dtype)


def _sb_block(q, k, v, q_start):
    tq, tk = q.shape[1], k.shape[1]
    z = jnp.einsum('bqhd,bkhd->bhqk', q, k).astype(jnp.float32) * (SB_DIM ** -0.5)
    mask = jnp.arange(tk)[None, :] < (q_start + jnp.arange(tq))[:, None]
    log_beta = jax.nn.log_sigmoid(z)
    log_stay = jnp.where(mask, jax.nn.log_sigmoid(-z), 0.0)
    later = lax.cumsum(log_stay, axis=3, reverse=True) - log_stay
    a = jnp.where(mask, jnp.exp(log_beta + later), 0.0)
    return jnp.einsum('bhqk,bkhd->bqhd', a.astype(v.dtype), v)


def _stick_breaking(q, k, v, k_past, v_past, prompt):
    if prompt:
        t = q.shape[1]
        outs = []
        for start in range(0, t, SB_BLOCK):
            end = start + SB_BLOCK
            outs.append(_sb_block(q[:, start:end], k[:, :end], v[:, :end], start))
        return jnp.concatenate(outs, axis=1)
    kk = jnp.concatenate([k_past.astype(k.dtype), k], axis=1)
    vv = jnp.concatenate([v_past.astype(v.dtype), v], axis=1)
    return _sb_block(q, kk, vv, k_past.shape[1])


def _causal_dwconv(hid, prev, w, b):
    t = hid.shape[1]
    hp = jnp.concatenate([prev.astype(hid.dtype), hid], axis=1)
    out = b + w[0] * hp[:, 0:t]
    for i in range(1, CONV_W):
        out = out + w[i] * hp[:, i:i + t]
    return out, hp[:, t:]


def _layer(x, lp, k_past, v_past, gla_s0, conv_prev, prompt):
    bsz, t, _ = x.shape
    hd = lambda a, n: a.reshape(bsz, t, n, -1)
    h = _rms_norm(x, lp['norm1_g'])
    cu, cv, gq, gk, gv, gr, glr, sq, sk, sv = _split_in(h @ lp['w_in'])
    oa, cm_v = _chunk_mlp(hd(cu, CM_GROUPS), hd(cv, CM_GROUPS), lp['cm_ws'], lp['cm_b'], prompt)
    log_a = jax.nn.log_sigmoid((glr @ lp['gla_w2'] + lp['gla_b']).astype(jnp.float32)) / GLA_TAU
    ob, gla_s = _gla(hd(gq, GLA_HEADS) * (GLA_DK ** -0.5), hd(gk, GLA_HEADS), hd(gv, GLA_HEADS),
                     hd(log_a, GLA_HEADS), gla_s0, CHUNK if prompt else t)
    ob = _rms_norm(ob, lp['gla_norm_g']) * jax.nn.silu(hd(gr, GLA_HEADS))
    q = _rms_norm(hd(sq, SB_HEADS), lp['sb_q_g'])
    k = _rms_norm(hd(sk, SB_HEADS), lp['sb_k_g'])
    v = hd(sv, SB_HEADS)
    oc = _stick_breaking(q, k, v, k_past, v_past, prompt)
    mix = jnp.concatenate([oa.reshape(bsz, t, -1), ob.reshape(bsz, t, -1),
                           oc.reshape(bsz, t, -1)], axis=-1)
    x = x + mix @ lp['w_out']
    h2 = _rms_norm(x, lp['norm2_g'])
    g, conv_state = _causal_dwconv(h2 @ lp['ffn_w_gate'], conv_prev, lp['ffn_conv_w'], lp['ffn_conv_b'])
    x = x + (jax.nn.silu(g) * (h2 @ lp['ffn_w_up'])) @ lp['ffn_w_down']
    return x, k, v, gla_s, conv_state, cm_v


def setup_inputs(seed: int = 0) -> dict:
    key = jax.random.key(seed)
    ks = jax.random.split(key, 24)
    nrm = lambda k, shape, scale: jax.random.normal(k, shape, jnp.float32) * scale
    return {
        'x_prompt': nrm(ks[0], (BATCH, SEQ, D_MODEL), 1.0),
        'x_sample': nrm(ks[1], (DEC_BATCH, DEC_SEQ, D_MODEL), 1.0),
        'cache_sb_k': nrm(ks[2], (DEPTH, DEC_BATCH, PAST_LEN, SB_HEADS, SB_DIM), 1.0),
        'cache_sb_v': nrm(ks[3], (DEPTH, DEC_BATCH, PAST_LEN, SB_HEADS, SB_DIM), 1.0),
        'state_gla': nrm(ks[4], (DEPTH, DEC_BATCH, GLA_HEADS, GLA_DK, GLA_DV), 1.0),
        'state_ffn_conv': nrm(ks[5], (DEPTH, DEC_BATCH, CONV_W - 1, D_FF), 1.0),
        'norm1_g': 1.0 + nrm(ks[6], (DEPTH, D_MODEL), 0.01),
        'w_in': nrm(ks[7], (DEPTH, D_MODEL, IN_W), D_MODEL ** -0.5),
        'cm_ws': nrm(ks[8], (DEPTH, CM_GROUPS, CM_BLOCK, CM_BLOCK), CM_BLOCK ** -0.5),
        'cm_b': 1.0 + nrm(ks[9], (DEPTH, CM_GROUPS, CM_BLOCK), 0.01),
        'gla_w2': nrm(ks[10], (DEPTH, GLA_RANK, GLA_K_W), GLA_RANK ** -0.5),
        'gla_b': nrm(ks[11], (DEPTH, GLA_K_W), 0.01),
        'gla_norm_g': 1.0 + nrm(ks[12], (DEPTH, GLA_HEADS, GLA_DV), 0.01),
        'sb_q_g': 1.0 + nrm(ks[13], (DEPTH, SB_DIM), 0.01),
        'sb_k_g': 1.0 + nrm(ks[14], (DEPTH, SB_DIM), 0.01),
        'w_out': nrm(ks[15], (DEPTH, MIX_W, D_MODEL), MIX_W ** -0.5),
        'norm2_g': 1.0 + nrm(ks[16], (DEPTH, D_MODEL), 0.01),
        'ffn_w_gate': nrm(ks[17], (DEPTH, D_MODEL, D_FF), D_MODEL ** -0.5),
        'ffn_w_up': nrm(ks[18], (DEPTH, D_MODEL, D_FF), D_MODEL ** -0.5),
        'ffn_conv_w': nrm(ks[19], (DEPTH, CONV_W, D_FF), CONV_W ** -0.5),
        'ffn_conv_b': nrm(ks[20], (DEPTH, D_FF), 0.01),
        'ffn_w_down': nrm(ks[21], (DEPTH, D_FF, D_MODEL), D_FF ** -0.5),
    }


def reference(x_prompt, x_sample, cache_sb_k, cache_sb_v, state_gla, state_ffn_conv,
              norm1_g, w_in, cm_ws, cm_b, gla_w2, gla_b, gla_norm_g, sb_q_g, sb_k_g,
              w_out, norm2_g, ffn_w_gate, ffn_w_up, ffn_conv_w, ffn_conv_b, ffn_w_down):
    yp, ys = x_prompt, x_sample
    bp = x_prompt.shape[0]
    kp_l, vp_l, sp_l, cp_l = [], [], [], []
    ks_l, vs_l, ss_l, cs_l, cmv_l = [], [], [], [], []
    for l in range(DEPTH):
        lp = {'norm1_g': norm1_g[l], 'w_in': w_in[l], 'cm_ws': cm_ws[l], 'cm_b': cm_b[l],
              'gla_w2': gla_w2[l], 'gla_b': gla_b[l], 'gla_norm_g': gla_norm_g[l],
              'sb_q_g': sb_q_g[l], 'sb_k_g': sb_k_g[l], 'w_out': w_out[l], 'norm2_g': norm2_g[l],
              'ffn_w_gate': ffn_w_gate[l], 'ffn_w_up': ffn_w_up[l], 'ffn_conv_w': ffn_conv_w[l],
              'ffn_conv_b': ffn_conv_b[l], 'ffn_w_down': ffn_w_down[l]}
        gla0 = jnp.zeros((bp, GLA_HEADS, GLA_DK, GLA_DV), jnp.float32)
        conv0 = jnp.zeros((bp, CONV_W - 1, D_FF), x_prompt.dtype)
        yp, kp, vp, sp, cp, _ = _layer(yp, lp, None, None, gla0, conv0, True)
        ys, ksn, vsn, ssn, csn, cmv = _layer(ys, lp, cache_sb_k[l], cache_sb_v[l],
                                             state_gla[l], state_ffn_conv[l], False)
        kp_l.append(kp); vp_l.append(vp); sp_l.append(sp); cp_l.append(cp)
        ks_l.append(ksn); vs_l.append(vsn); ss_l.append(ssn); cs_l.append(csn); cmv_l.append(cmv)
    sb_k_prompt = jnp.stack(kp_l)
    sb_v_prompt = jnp.stack(vp_l)
    gla_prompt = jnp.stack(sp_l)
    conv_prompt = jnp.stack(cp_l)
    sb_k_sample = jnp.stack(ks_l)
    sb_v_sample = jnp.stack(vs_l)
    gla_sample = jnp.stack(ss_l)
    conv_sample = jnp.stack(cs_l)
    cm_v_sample = jnp.stack(cmv_l)
    return (yp, ys, sb_k_prompt, sb_v_prompt, gla_prompt, conv_prompt,
            sb_k_sample, sb_v_sample, gla_sample, conv_sample, cm_v_sample)
```

```python
import functools

import jax
import jax.numpy as jnp
from jax import lax
from jax.experimental import pallas as pl
from jax.experimental.pallas import tpu as pltpu

F32 = jnp.float32
BF16 = jnp.bfloat16

D_MODEL = 2048
CHUNK = 64
CM_BLOCK = 128
CM_GROUPS = 4
CM_DIM = 128
CM_W = CM_GROUPS * CM_DIM
GLA_HEADS = 4
GLA_DK = 64
GLA_DV = 128
GLA_RANK = 16
GLA_TAU = 16.0
GLA_V_W = GLA_HEADS * GLA_DV
SB_HEADS = 8
SB_DIM = 128
SB_W = SB_HEADS * SB_DIM
D_FF = 5632
CONV_W = 3
EPS = 1e-6

LANES = 128
V7X_VMEM_LIMIT = 56 * 2 ** 20

COL_CM_U = 0
COL_CM_V = 512
COL_GQ = 1024
COL_GK = 1536
COL_GV = 2048
COL_GR = 2560
COL_GLR = 3072
COL_SQ = 3328
COL_SK = 4352
PROJ_W = 5376

SB_LOG_ZERO = -105.0


def _dot(a, b):
    return jnp.dot(a, b, preferred_element_type=F32)


def _dot_nt(a, b):
    return lax.dot_general(a, b, (((1,), (1,)), ((), ())), preferred_element_type=F32)


def _dot_tn(a, b):
    return lax.dot_general(a, b, (((0,), (0,)), ((), ())), preferred_element_type=F32)


def _rms(x, g):
    ms = jnp.mean(x * x, axis=-1, keepdims=True)
    return x * lax.rsqrt(ms + EPS) * g


def _gelu_tanh(x):
    return x * (0.5 * (1.0 + jnp.tanh(0.7978845608028654 * (x + 0.044715 * (x * x * x)))))


def _log_sigmoid(x):
    return jnp.minimum(x, 0.0) - jnp.log1p(jnp.exp(-jnp.abs(x)))


def _silu(x):
    return x * (1.0 / (1.0 + jnp.exp(-x)))


def _split_bf16(x):
    hi = x.astype(BF16)
    lo = (x - hi.astype(F32)).astype(BF16)
    return hi, lo


def _params(*sem):
    return pltpu.CompilerParams(dimension_semantics=sem, vmem_limit_bytes=V7X_VMEM_LIMIT)


def _norm_matmul_kernel(x_ref, g_ref, w_ref, o_ref, h_ref):
    @pl.when(pl.program_id(1) == 0)
    def _():
        h_ref[...] = _rms(x_ref[...], g_ref[...]).astype(BF16)

    o_ref[...] = _dot(h_ref[...], w_ref[...])


def _norm_matmul(x, g, w, tm, tn):
    n, d = x.shape
    nout = w.shape[1]
    return pl.pallas_call(
        _norm_matmul_kernel,
        out_shape=jax.ShapeDtypeStruct((n, nout), F32),
        grid=(n // tm, nout // tn),
        in_specs=[pl.BlockSpec((tm, d), lambda i, j: (i, 0)),
                  pl.BlockSpec((1, d), lambda i, j: (0, 0)),
                  pl.BlockSpec((d, tn), lambda i, j: (0, j))],
        out_specs=pl.BlockSpec((tm, tn), lambda i, j: (i, j)),
        scratch_shapes=[pltpu.VMEM((tm, d), BF16)],
        compiler_params=_params("parallel", "arbitrary"),
        name="norm_matmul",
    )(x, g, w)


def _chunk_mlp_kernel(u_ref, v_ref, ws_ref, b_ref, o_ref, *vn_refs, chunk_mask, nblk):
    row = lax.broadcasted_iota(jnp.int32, (CM_BLOCK, CM_BLOCK), 0)
    col = lax.broadcasted_iota(jnp.int32, (CM_BLOCK, CM_BLOCK), 1)
    visible = jnp.logical_or(row >= CHUNK, col < CHUNK)
    for g in range(CM_GROUPS):
        w = ws_ref[g]
        if chunk_mask:
            w = jnp.where(visible, w, 0.0)
        w = w.astype(BF16)
        bias = b_ref[g]
        cols = slice(g * CM_DIM, (g + 1) * CM_DIM)
        for n in range(nblk):
            rows = slice(n * CM_BLOCK, (n + 1) * CM_BLOCK)
            v = _gelu_tanh(v_ref[rows, cols])
            d = v - jnp.mean(v, axis=-1, keepdims=True)
            vn = d * lax.rsqrt(jnp.mean(d * d, axis=-1, keepdims=True) + EPS)
            if vn_refs:
                vn_refs[0][rows, cols] = vn
            s = _dot(w, vn.astype(BF16)) + bias
            o_ref[rows, cols] = (_gelu_tanh(u_ref[rows, cols]) * s).astype(BF16)


def _chunk_mlp(proj, ws, b, *, chunk_mask, want_vn, tm):
    n = proj.shape[0]
    nblk = tm // CM_BLOCK
    out_shape = [jax.ShapeDtypeStruct((n, CM_W), BF16)]
    out_specs = [pl.BlockSpec((tm, CM_W), lambda i: (i, 0))]
    if want_vn:
        out_shape.append(jax.ShapeDtypeStruct((n, CM_W), F32))
        out_specs.append(pl.BlockSpec((tm, CM_W), lambda i: (i, 0)))
    return pl.pallas_call(
        functools.partial(_chunk_mlp_kernel, chunk_mask=chunk_mask, nblk=nblk),
        out_shape=out_shape,
        grid=(n // tm,),
        in_specs=[pl.BlockSpec((tm, CM_W), lambda i: (i, COL_CM_U // CM_W)),
                  pl.BlockSpec((tm, CM_W), lambda i: (i, COL_CM_V // CM_W)),
                  pl.BlockSpec((CM_GROUPS, CM_BLOCK, CM_BLOCK), lambda i: (0, 0, 0)),
                  pl.BlockSpec((CM_GROUPS, CM_BLOCK, 1), lambda i: (0, 0, 0))],
        out_specs=out_specs,
        compiler_params=_params("parallel"),
        name="chunk_mlp",
    )(proj, proj, ws, b)


def _gla_kernel(q_ref, k_ref, v_ref, r_ref, glr_ref, w2_ref, b_ref, gn_ref, s0_ref,
                o_ref, st_ref, s_scr, *, chunk, nchunk):
    c = pl.program_id(1)

    @pl.when(c == 0)
    def _():
        s_scr[...] = s0_ref[0]

    row = lax.broadcasted_iota(jnp.int32, (chunk, chunk), 0)
    col = lax.broadcasted_iota(jnp.int32, (chunk, chunk), 1)
    causal = col <= row
    tril = jnp.where(causal, 1.0, 0.0).astype(BF16)

    def body(ci, carry):
        rows = pl.ds(pl.multiple_of(ci * chunk, chunk), chunk)
        x = _dot(glr_ref[rows, :].astype(BF16), w2_ref[...]) + b_ref[...]
        log_a = _log_sigmoid(x) * (1.0 / GLA_TAU)
        hi, lo = _split_bf16(log_a)
        cum = _dot(tril, hi) + _dot(tril, lo)
        last = cum[chunk - 1:chunk, :]
        for h in range(GLA_HEADS):
            cs = slice(h * LANES, (h + 1) * LANES)
            cum_h = cum[:, cs]
            last_h = last[:, cs]
            q = q_ref[rows, cs]
            k = k_ref[rows, cs]
            v = v_ref[rows, cs].astype(BF16)
            qe = ((q * (GLA_DK ** -0.5)) * jnp.exp(cum_h)).astype(BF16)
            ke = (k * jnp.exp(-cum_h)).astype(BF16)
            att = jnp.where(causal, _dot_nt(qe, ke), 0.0).astype(BF16)
            st = s_scr[h]
            o = _dot(att, v) + _dot_nt(qe, st.astype(BF16))
            kd = (k * jnp.exp(last_h - cum_h)).astype(BF16)
            s_scr[h] = st * jnp.exp(last_h) + _dot_tn(v, kd)
            on = _rms(o, gn_ref[:, cs])
            o_ref[rows, cs] = (on * _silu(r_ref[rows, cs])).astype(BF16)
        return carry

    lax.fori_loop(0, nchunk, body, 0)

    @pl.when(c == pl.num_programs(1) - 1)
    def _():
        st_ref[0] = s_scr[...]


def _gla(proj, w2, b, gn, s0t, *, nseq, seq_len, chunk, rows_per_step):
    n = proj.shape[0]
    steps = seq_len // rows_per_step
    nchunk = rows_per_step // chunk
    w = GLA_V_W

    def rowmap(colblk):
        return lambda s, c: (s * steps + c, colblk)

    return pl.pallas_call(
        functools.partial(_gla_kernel, chunk=chunk, nchunk=nchunk),
        out_shape=[jax.ShapeDtypeStruct((n, w), BF16),
                   jax.ShapeDtypeStruct((nseq, GLA_HEADS, LANES, LANES), F32)],
        grid=(nseq, steps),
        in_specs=[pl.BlockSpec((rows_per_step, w), rowmap(COL_GQ // w)),
                  pl.BlockSpec((rows_per_step, w), rowmap(COL_GK // w)),
                  pl.BlockSpec((rows_per_step, w), rowmap(COL_GV // w)),
                  pl.BlockSpec((rows_per_step, w), rowmap(COL_GR // w)),
                  pl.BlockSpec((rows_per_step, LANES), rowmap(COL_GLR // LANES)),
                  pl.BlockSpec((LANES, w), lambda s, c: (0, 0)),
                  pl.BlockSpec((1, w), lambda s, c: (0, 0)),
                  pl.BlockSpec((1, w), lambda s, c: (0, 0)),
                  pl.BlockSpec((1, GLA_HEADS, LANES, LANES), lambda s, c: (s, 0, 0, 0))],
        out_specs=[pl.BlockSpec((rows_per_step, w), lambda s, c: (s * steps + c, 0)),
                   pl.BlockSpec((1, GLA_HEADS, LANES, LANES), lambda s, c: (s, 0, 0, 0))],
        scratch_shapes=[pltpu.VMEM((GLA_HEADS, LANES, LANES), F32)],
        compiler_params=_params("parallel", "arbitrary"),
        name="gla",
    )(proj, proj, proj, proj, proj, w2, b, gn, s0t)


def _head_norm_kernel(x_ref, g_ref, o_ref):
    o_ref[...] = _rms(x_ref[...], g_ref[...])


def _head_norm(proj, g, colblk0, tm):
    n = proj.shape[0]
    return pl.pallas_call(
        _head_norm_kernel,
        out_shape=jax.ShapeDtypeStruct((n, SB_W), F32),
        grid=(n // tm, SB_HEADS),
        in_specs=[pl.BlockSpec((tm, SB_DIM), lambda i, h: (i, colblk0 + h)),
                  pl.BlockSpec((1, SB_DIM), lambda i, h: (0, 0))],
        out_specs=pl.BlockSpec((tm, SB_DIM), lambda i, h: (i, h)),
        compiler_params=_params("parallel", "parallel"),
        name="head_norm",
    )(proj, g)


def _sb_kernel(q_ref, gq_ref, kd_ref, vd_ref, kp_ref, vp_ref, uo_ref, o_ref, *, tq, n_past):
    scale = SB_DIM ** -0.5
    qn = _rms(q_ref[...], gq_ref[...]).astype(BF16)
    uo = uo_ref[...]

    def block(k, v, carry, mask):
        z = _dot_nt(qn, k) * scale
        log_beta = _log_sigmoid(z)
        log_stay = log_beta - z
        if mask is not None:
            log_stay = jnp.where(mask, log_stay, 0.0)
        hi, lo = _split_bf16(log_stay)
        sums = _dot(hi, uo) + _dot(lo, uo)
        a = jnp.exp(log_beta + (carry + sums[:, :SB_BLK]))
        if mask is not None:
            a = jnp.where(mask, a, 0.0)
        return _dot(a.astype(BF16), v), carry + sums[:, SB_BLK:]

    row = lax.broadcasted_iota(jnp.int32, (tq, SB_BLK), 0)
    col = lax.broadcasted_iota(jnp.int32, (tq, SB_BLK), 1)
    acc, carry = block(kd_ref[...].astype(BF16), vd_ref[...].astype(BF16),
                       jnp.zeros((tq, SB_BLK), F32), col < row)

    def cond(state):
        kb, mx, _, _ = state
        return jnp.logical_and(kb >= 0, mx > SB_LOG_ZERO)

    def body(state):
        kb, _, carry, acc = state
        rows = pl.ds(pl.multiple_of(kb * SB_BLK, SB_BLK), SB_BLK)
        out, carry = block(kp_ref[rows, :].astype(BF16), vp_ref[rows, :].astype(BF16), carry, None)
        return kb - 1, jnp.max(carry), carry, acc + out

    last = (pl.program_id(2) if n_past is None else n_past) - 1
    _, _, _, acc = lax.while_loop(cond, body, (last, jnp.max(carry), carry, acc))
    o_ref[...] = acc.astype(BF16)


SB_BLK = 128


def _sb_attention(q_arr, q_col0, gq, kd_arr, vd_arr, kp_arr, vp_arr, uo, *, nseq, tq, qtiles, past_rows, n_past):
    n = q_arr.shape[0]
    return pl.pallas_call(
        functools.partial(_sb_kernel, tq=tq, n_past=n_past),
        out_shape=jax.ShapeDtypeStruct((n, SB_W), BF16),
        grid=(nseq, SB_HEADS, qtiles),
        in_specs=[pl.BlockSpec((tq, SB_DIM), lambda s, h, i: (s * qtiles + i, q_col0 + h)),
                  pl.BlockSpec((1, SB_DIM), lambda s, h, i: (0, 0)),
                  pl.BlockSpec((SB_BLK, SB_DIM), lambda s, h, i: (s * qtiles + i, h)),
                  pl.BlockSpec((SB_BLK, SB_DIM), lambda s, h, i: (s * qtiles + i, h)),
                  pl.BlockSpec((past_rows, SB_DIM), lambda s, h, i: (s, h)),
                  pl.BlockSpec((past_rows, SB_DIM), lambda s, h, i: (s, h)),
                  pl.BlockSpec((SB_BLK, 2 * SB_BLK), lambda s, h, i: (0, 0))],
        out_specs=pl.BlockSpec((tq, SB_DIM), lambda s, h, i: (s * qtiles + i, h)),
        compiler_params=_params("parallel", "parallel", "arbitrary"),
        name="sb_attention",
    )(q_arr, gq, kd_arr, vd_arr, kp_arr, vp_arr, uo)


def _out_proj_kernel(x_ref, a_ref, b_ref, c_ref, w_ref, o_ref):
    o_ref[...] = (x_ref[...]
                  + _dot(a_ref[...], w_ref[0:CM_W, :])
                  + _dot(b_ref[...], w_ref[CM_W:CM_W + GLA_V_W, :])
                  + _dot(c_ref[...], w_ref[CM_W + GLA_V_W:, :]))


def _out_proj(x, oa, ob, oc, w, tm):
    n, d = x.shape
    return pl.pallas_call(
        _out_proj_kernel,
        out_shape=jax.ShapeDtypeStruct((n, d), F32),
        grid=(n // tm,),
        in_specs=[pl.BlockSpec((tm, d), lambda i: (i, 0)),
                  pl.BlockSpec((tm, CM_W), lambda i: (i, 0)),
                  pl.BlockSpec((tm, GLA_V_W), lambda i: (i, 0)),
                  pl.BlockSpec((tm, SB_W), lambda i: (i, 0)),
                  pl.BlockSpec(w.shape, lambda i: (0, 0))],
        out_specs=pl.BlockSpec((tm, d), lambda i: (i, 0)),
        compiler_params=_params("parallel"),
        name="out_proj",
    )(x, oa, ob, oc, w)


HALO = 16


def _ffn_conv(gate, prev1, prev2, cw_ref, cb_ref):
    return ((cb_ref[...] + cw_ref[0:1, :] * prev2) + cw_ref[1:2, :] * prev1) + cw_ref[2:3, :] * gate


def _ffn_prompt_kernel(x_ref, xp_ref, g_ref, wg_ref, wu_ref, cw_ref, cb_ref, wd_ref,
                       o_ref, tail_ref, h_ref, *, tm, tiles_per_seq):
    i = pl.program_id(0)

    @pl.when(pl.program_id(1) == 0)
    def _():
        x = x_ref[...]
        h_ref[HALO:, :] = _rms(x, g_ref[...]).astype(BF16)
        h_ref[0:HALO, :] = _rms(xp_ref[...], g_ref[...]).astype(BF16)
        o_ref[...] = x

    h = h_ref[...]
    gate_all = _dot(h, wg_ref[...])
    first = (i % tiles_per_seq) == 0
    rowid = lax.broadcasted_iota(jnp.int32, gate_all.shape, 0)
    gate_all = jnp.where(jnp.logical_and(first, rowid < HALO), 0.0, gate_all)
    prev1 = pltpu.roll(gate_all, 1, 0)[HALO:, :]
    prev2 = pltpu.roll(gate_all, 2, 0)[HALO:, :]
    gate = gate_all[HALO:, :]
    g = _ffn_conv(gate, prev1, prev2, cw_ref, cb_ref)
    up = _dot(h[HALO:, :], wu_ref[...])
    o_ref[...] += _dot((_silu(g) * up).astype(BF16), wd_ref[...])
    tail_ref[0] = gate[tm - 8:, :]


def _ffn_prompt(x, g, wg, wu, cw, cb, wd, *, seq_len, tm, tf):
    n, d = x.shape
    dff = wg.shape[1]
    tiles_per_seq = seq_len // tm
    halo_blocks = tm // HALO
    return pl.pallas_call(
        functools.partial(_ffn_prompt_kernel, tm=tm, tiles_per_seq=tiles_per_seq),
        out_shape=[jax.ShapeDtypeStruct((n, d), F32),
                   jax.ShapeDtypeStruct((n // tm, 8, dff), F32)],
        grid=(n // tm, dff // tf),
        in_specs=[pl.BlockSpec((tm, d), lambda i, j: (i, 0)),
                  pl.BlockSpec((HALO, d), lambda i, j: (jnp.maximum(i * halo_blocks - 1, 0), 0)),
                  pl.BlockSpec((1, d), lambda i, j: (0, 0)),
                  pl.BlockSpec((d, tf), lambda i, j: (0, j)),
                  pl.BlockSpec((d, tf), lambda i, j: (0, j)),
                  pl.BlockSpec((CONV_W, tf), lambda i, j: (0, j)),
                  pl.BlockSpec((1, tf), lambda i, j: (0, j)),
                  pl.BlockSpec((tf, d), lambda i, j: (j, 0))],
        out_specs=[pl.BlockSpec((tm, d), lambda i, j: (i, 0)),
                   pl.BlockSpec((1, 8, tf), lambda i, j: (i, 0, j))],
        scratch_shapes=[pltpu.VMEM((HALO + tm, d), BF16)],
        compiler_params=_params("parallel", "arbitrary"),
        name="ffn_prompt",
    )(x, x, g, wg, wu, cw, cb, wd)


def _ffn_sample_kernel(x_ref, p1_ref, p2_ref, g_ref, wg_ref, wu_ref, cw_ref, cb_ref, wd_ref,
                       o_ref, gate_ref, h_ref, *, seq_len):
    @pl.when(pl.program_id(0) == 0)
    def _():
        x = x_ref[...]
        h_ref[...] = _rms(x, g_ref[...]).astype(BF16)
        o_ref[...] = x

    h = h_ref[...]
    gate = _dot(h, wg_ref[...])
    t = lax.broadcasted_iota(jnp.int32, gate.shape, 0) % seq_len
    prev1 = jnp.where(t >= 1, pltpu.roll(gate, 1, 0), p1_ref[...])
    prev2 = jnp.where(t >= 2, pltpu.roll(gate, 2, 0), p2_ref[...])
    g = _ffn_conv(gate, prev1, prev2, cw_ref, cb_ref)
    up = _dot(h, wu_ref[...])
    o_ref[...] += _dot((_silu(g) * up).astype(BF16), wd_ref[...])
    gate_ref[...] = gate


def _ffn_sample(x, p1, p2, g, wg, wu, cw, cb, wd, *, seq_len, tf):
    n, d = x.shape
    dff = wg.shape[1]
    return pl.pallas_call(
        functools.partial(_ffn_sample_kernel, seq_len=seq_len),
        out_shape=[jax.ShapeDtypeStruct((n, d), F32),
                   jax.ShapeDtypeStruct((n, dff), F32)],
        grid=(dff // tf,),
        in_specs=[pl.BlockSpec((n, d), lambda j: (0, 0)),
                  pl.BlockSpec((n, tf), lambda j: (0, j)),
                  pl.BlockSpec((n, tf), lambda j: (0, j)),
                  pl.BlockSpec((1, d), lambda j: (0, 0)),
                  pl.BlockSpec((d, tf), lambda j: (0, j)),
                  pl.BlockSpec((d, tf), lambda j: (0, j)),
                  pl.BlockSpec((CONV_W, tf), lambda j: (0, j)),
                  pl.BlockSpec((1, tf), lambda j: (0, j)),
                  pl.BlockSpec((tf, d), lambda j: (j, 0))],
        out_specs=[pl.BlockSpec((n, d), lambda j: (0, 0)),
                   pl.BlockSpec((n, tf), lambda j: (0, j))],
        scratch_shapes=[pltpu.VMEM((n, d), BF16)],
        compiler_params=_params("arbitrary"),
        name="ffn_sample",
    )(x, p1, p2, g, wg, wu, cw, cb, wd)


def _pack_w_in(w_in):
    nl, d, _ = w_in.shape
    o = 0
    seg = {}
    for name, width in (("cu", CM_W), ("cv", CM_W), ("gq", GLA_HEADS * GLA_DK), ("gk", GLA_HEADS * GLA_DK),
                        ("gv", GLA_V_W), ("gr", GLA_V_W), ("glr", GLA_RANK), ("sq", SB_W), ("sk", SB_W), ("sv", SB_W)):
        seg[name] = w_in[:, :, o:o + width]
        o += width

    def pad_heads(w):
        w = w.reshape(nl, d, GLA_HEADS, GLA_DK)
        return jnp.pad(w, ((0, 0), (0, 0), (0, 0), (0, LANES - GLA_DK))).reshape(nl, d, GLA_HEADS * LANES)

    glr = jnp.pad(seg["glr"], ((0, 0), (0, 0), (0, COL_SQ - COL_GLR - GLA_RANK)))
    main = jnp.concatenate([seg["cu"], seg["cv"], pad_heads(seg["gq"]), pad_heads(seg["gk"]),
                            seg["gv"], seg["gr"], glr, seg["sq"], seg["sk"]], axis=-1)
    return main.astype(BF16), seg["sv"].astype(BF16)


def _pack_gla_gate(gla_w2, gla_b):
    nl = gla_w2.shape[0]
    w2 = gla_w2.reshape(nl, GLA_RANK, GLA_HEADS, GLA_DK)
    w2 = jnp.pad(w2, ((0, 0), (0, LANES - GLA_RANK), (0, 0), (0, LANES - GLA_DK)))
    b = jnp.pad(gla_b.reshape(nl, 1, GLA_HEADS, GLA_DK), ((0, 0), (0, 0), (0, 0), (0, LANES - GLA_DK)))
    return w2.reshape(nl, LANES, GLA_HEADS * LANES).astype(BF16), b.reshape(nl, 1, GLA_HEADS * LANES)


def _state_to_kernel(s):
    st = jnp.swapaxes(s, 2, 3)
    return jnp.pad(st, ((0, 0), (0, 0), (0, 0), (0, LANES - GLA_DK)))


def _state_from_kernel(st):
    return jnp.swapaxes(st, 2, 3)[:, :, :GLA_DK, :]


def _later_sum_matrix():
    j = lax.broadcasted_iota(jnp.int32, (SB_BLK, 2 * SB_BLK), 0)
    s = lax.broadcasted_iota(jnp.int32, (SB_BLK, 2 * SB_BLK), 1)
    return jnp.where(jnp.logical_or(j > s, s >= SB_BLK), 1.0, 0.0).astype(BF16)


def _mixers_and_ffn_prompt(x, lw, l, *, nseq, seq_len, uo):
    tm = 512
    proj = _norm_matmul(x, lw["norm1_g"][l], lw["w_main"][l], tm, 768)
    v = _norm_matmul(x, lw["norm1_g"][l], lw["w_sv"][l], tm, SB_W)
    (oa,) = _chunk_mlp(proj, lw["cm_ws"][l], lw["cm_b"][l], chunk_mask=True, want_vn=False, tm=tm)
    s0t = jnp.zeros((nseq, GLA_HEADS, LANES, LANES), F32)
    ob, st = _gla(proj, lw["gla_w2"][l], lw["gla_b"][l], lw["gla_gn"][l], s0t,
                  nseq=nseq, seq_len=seq_len, chunk=CHUNK, rows_per_step=tm)
    kn = _head_norm(proj, lw["sb_k_g"][l], COL_SK // SB_DIM, 1024)
    oc = _sb_attention(proj, COL_SQ // SB_DIM, lw["sb_q_g"][l], kn, v, kn, v, uo,
                       nseq=nseq, tq=SB_BLK, qtiles=seq_len // SB_BLK, past_rows=seq_len, n_past=None)
    x = _out_proj(x, oa, ob, oc, lw["w_out"][l], tm)
    x, tail = _ffn_prompt(x, lw["norm2_g"][l], lw["wg"][l], lw["wu"][l], lw["conv_w"][l], lw["conv_b"][l],
                          lw["wd"][l], seq_len=seq_len, tm=tm, tf=512)
    tiles_per_seq = seq_len // tm
    conv_state = tail[tiles_per_seq - 1::tiles_per_seq, 8 - (CONV_W - 1):, :]
    return x, kn, v, _state_from_kernel(st), conv_state


def _mixers_and_ffn_sample(x, lw, l, cache_k, cache_v, gla_s0, conv_prev, *, nseq, seq_len, uo):
    n = x.shape[0]
    past_len = cache_k.shape[1]
    proj = _norm_matmul(x, lw["norm1_g"][l], lw["w_main"][l], n, 768)
    v = _norm_matmul(x, lw["norm1_g"][l], lw["w_sv"][l], n, SB_W)
    ws = lw["cm_ws"][l][:, :seq_len, :seq_len]
    ws_bd = jnp.einsum("ab,gij->gaibj", jnp.eye(nseq, dtype=F32), ws).reshape(CM_GROUPS, n, n)
    b_bd = jnp.tile(lw["cm_b"][l][:, :seq_len, :], (1, nseq, 1))
    oa, cm_v = _chunk_mlp(proj, ws_bd, b_bd, chunk_mask=False, want_vn=True, tm=n)
    ob, st = _gla(proj, lw["gla_w2"][l], lw["gla_b"][l], lw["gla_gn"][l], _state_to_kernel(gla_s0),
                  nseq=nseq, seq_len=seq_len, chunk=seq_len, rows_per_step=seq_len)
    kn = _head_norm(proj, lw["sb_k_g"][l], COL_SK // SB_DIM, n)

    def pad_new(a):
        a = jnp.pad(a.reshape(nseq, seq_len, SB_W), ((0, 0), (0, SB_BLK - seq_len), (0, 0)))
        return a.reshape(nseq * SB_BLK, SB_W)

    oc = _sb_attention(proj, COL_SQ // SB_DIM, lw["sb_q_g"][l], pad_new(kn), pad_new(v),
                       cache_k.reshape(nseq * past_len, SB_W), cache_v.reshape(nseq * past_len, SB_W), uo,
                       nseq=nseq, tq=seq_len, qtiles=1, past_rows=past_len, n_past=past_len // SB_BLK)
    x = _out_proj(x, oa, ob, oc, lw["w_out"][l], n)
    dff = conv_prev.shape[-1]
    p = jnp.zeros((nseq, seq_len, dff), F32)
    p1 = p.at[:, 0].set(conv_prev[:, 1]).reshape(n, dff)
    p2 = p.at[:, 0].set(conv_prev[:, 0]).at[:, 1].set(conv_prev[:, 1]).reshape(n, dff)
    x, gate = _ffn_sample(x, p1, p2, lw["norm2_g"][l], lw["wg"][l], lw["wu"][l], lw["conv_w"][l],
                          lw["conv_b"][l], lw["wd"][l], seq_len=seq_len, tf=512)
    conv_state = gate.reshape(nseq, seq_len, dff)[:, seq_len - (CONV_W - 1):, :]
    return x, kn, v, _state_from_kernel(st), conv_state, cm_v


def kernel(x_prompt, x_sample, cache_sb_k, cache_sb_v, state_gla, state_ffn_conv, norm1_g, w_in, cm_ws, cm_b,
           gla_w2, gla_b, gla_norm_g, sb_q_g, sb_k_g, w_out, norm2_g, ffn_w_gate, ffn_w_up, ffn_conv_w,
           ffn_conv_b, ffn_w_down):
    depth = w_in.shape[0]
    bp, tp, d = x_prompt.shape
    bs, ts, _ = x_sample.shape
    assert d == D_MODEL and bs * ts == CM_BLOCK and tp % 512 == 0

    w_main, w_sv = _pack_w_in(w_in)
    w2, b2 = _pack_gla_gate(gla_w2, gla_b)
    lw = {
        "norm1_g": norm1_g[:, None, :], "w_main": w_main, "w_sv": w_sv,
        "cm_ws": cm_ws, "cm_b": cm_b[..., None],
        "gla_w2": w2, "gla_b": b2, "gla_gn": gla_norm_g.reshape(depth, 1, GLA_V_W),
        "sb_q_g": sb_q_g[:, None, :], "sb_k_g": sb_k_g[:, None, :],
        "w_out": w_out.astype(BF16), "norm2_g": norm2_g[:, None, :],
        "wg": ffn_w_gate.astype(BF16), "wu": ffn_w_up.astype(BF16), "wd": ffn_w_down.astype(BF16),
        "conv_w": ffn_conv_w, "conv_b": ffn_conv_b[:, None, :],
    }
    uo = _later_sum_matrix()

    yp = x_prompt.reshape(bp * tp, d)
    ys = x_sample.reshape(bs * ts, d)
    outs_p, outs_s = [], []
    for l in range(depth):
        yp, kp, vp, sp, cp = _mixers_and_ffn_prompt(yp, lw, l, nseq=bp, seq_len=tp, uo=uo)
        ys, ks, vs, ss, cs, cmv = _mixers_and_ffn_sample(
            ys, lw, l, cache_sb_k[l], cache_sb_v[l], state_gla[l], state_ffn_conv[l], nseq=bs, seq_len=ts, uo=uo)
        outs_p.append((kp.reshape(bp, tp, SB_HEADS, SB_DIM), vp.reshape(bp, tp, SB_HEADS, SB_DIM), sp, cp))
        outs_s.append((ks.reshape(bs, ts, SB_HEADS, SB_DIM), vs.reshape(bs, ts, SB_HEADS, SB_DIM), ss, cs,
                       cmv.reshape(bs, ts, CM_GROUPS, CM_DIM)))

    stack = lambda outs, i: jnp.stack([o[i] for o in outs])
    return (yp.reshape(bp, tp, d), ys.reshape(bs, ts, d),
            stack(outs_p, 0), stack(outs_p, 1), stack(outs_p, 2), stack(outs_p, 3),
            stack(outs_s, 0), stack(outs_s, 1), stack(outs_s, 2), stack(outs_s, 3), stack(outs_s, 4))
```

```python
import functools

import jax
import jax.numpy as jnp
from jax import lax
from jax.experimental import pallas as pl
from jax.experimental.pallas import tpu as pltpu

F32 = jnp.float32
BF16 = jnp.bfloat16

D_MODEL = 2048
CHUNK = 64
CM_BLOCK = 128
CM_GROUPS = 4
CM_DIM = 128
CM_W = CM_GROUPS * CM_DIM
GLA_HEADS = 4
GLA_DK = 64
GLA_DV = 128
GLA_RANK = 16
GLA_TAU = 16.0
GLA_V_W = GLA_HEADS * GLA_DV
SB_HEADS = 8
SB_DIM = 128
SB_W = SB_HEADS * SB_DIM
D_FF = 5632
CONV_W = 3
EPS = 1e-6

LANES = 128
V7X_VMEM_LIMIT = 56 * 2 ** 20

TM = 512
TN = 1024
TF = 512

COL_CM_U = 0
COL_CM_V = 512
COL_GQ = 1024
COL_GK = 1536
COL_GV = 2048
COL_GR = 2560
COL_SQ = 3072
PROJ_W = 4096
PROJ_TILES = PROJ_W // TN
KV_TILES = 2

SB_BLK = 128
SB_WINDOW = 2
SB_LOG_ZERO = -105.0
SB_DONE = -1e30


def _dot(a, b):
    return jnp.dot(a, b, preferred_element_type=F32)


def _dot_nt(a, b):
    return lax.dot_general(a, b, (((1,), (1,)), ((), ())), preferred_element_type=F32)


def _dot_tn(a, b):
    return lax.dot_general(a, b, (((0,), (0,)), ((), ())), preferred_element_type=F32)


def _rms(x, g):
    ms = jnp.mean(x * x, axis=-1, keepdims=True)
    return x * lax.rsqrt(ms + EPS) * g


def _gelu_tanh(x):
    return x * (0.5 * (1.0 + jnp.tanh(0.7978845608028654 * (x + 0.044715 * (x * x * x)))))


def _log_sigmoid(x):
    return jnp.minimum(x, 0.0) - jnp.log1p(jnp.exp(-jnp.abs(x)))


def _silu(x):
    return x * (1.0 / (1.0 + jnp.exp(-x)))


def _split_bf16(x):
    hi = x.astype(BF16)
    lo = (x - hi.astype(F32)).astype(BF16)
    return hi, lo


def _params(*sem):
    return pltpu.CompilerParams(dimension_semantics=sem, vmem_limit_bytes=V7X_VMEM_LIMIT)


def _head_cols(h):
    return slice(h * LANES, (h + 1) * LANES)


def _head_rows(h, ntok):
    return pl.ds(h, ntok, stride=SB_HEADS)


def _in_proj_kernel(*refs, n_alias):
    x_ref, g_ref, w_ref, wl_ref, gk_ref = refs[:5]
    proj_ref, k_ref, v_ref, glr_ref, h_ref = refs[5 + n_alias:]
    j = pl.program_id(1)

    @pl.when(j == 0)
    def _():
        h = _rms(x_ref[...], g_ref[...]).astype(BF16)
        h_ref[...] = h
        glr_ref[...] = _dot(h, wl_ref[...])

    @pl.when(j < PROJ_TILES)
    def _():
        proj_ref[...] = _dot(h_ref[...], w_ref[...])

    @pl.when(j == PROJ_TILES)
    def _():
        k = _dot(h_ref[...], w_ref[...])
        for h in range(SB_HEADS):
            k_ref[_head_rows(h, k.shape[0]), :] = _rms(k[:, _head_cols(h)], gk_ref[...])

    @pl.when(j == PROJ_TILES + 1)
    def _():
        v = _dot(h_ref[...], w_ref[...])
        for h in range(SB_HEADS):
            v_ref[_head_rows(h, v.shape[0]), :] = v[:, _head_cols(h)]


def _in_proj(x, g, w_all, w_glr, gk, l, tm, *, kv_rows, kv_row0, kv_bufs):
    n, d = x.shape
    blk0 = kv_row0 // tm
    in_specs = [pl.BlockSpec((tm, d), lambda i, j: (i, 0)),
                pl.BlockSpec((None, 1, d), lambda i, j: (l, 0, 0)),
                pl.BlockSpec((None, d, TN), lambda i, j: (l, 0, j)),
                pl.BlockSpec((None, d, LANES), lambda i, j: (l, 0, 0)),
                pl.BlockSpec((None, 1, SB_DIM), lambda i, j: (l, 0, 0))]
    args = [x, g, w_all, w_glr, gk]
    aliases = {}
    if kv_bufs is not None:
        in_specs += [pl.BlockSpec(memory_space=pl.ANY)] * 2
        args += list(kv_bufs)
        aliases = {5: 1, 6: 2}
    kv_shape = jax.ShapeDtypeStruct((kv_rows * SB_HEADS, SB_DIM), F32)
    kv_spec = pl.BlockSpec((tm * SB_HEADS, SB_DIM), lambda i, j: (blk0 + i, 0))
    return pl.pallas_call(
        functools.partial(_in_proj_kernel, n_alias=len(aliases)),
        out_shape=[jax.ShapeDtypeStruct((n, PROJ_W), F32), kv_shape, kv_shape,
                   jax.ShapeDtypeStruct((n, LANES), F32)],
        grid=(n // tm, PROJ_TILES + KV_TILES),
        in_specs=in_specs,
        out_specs=[pl.BlockSpec((tm, TN), lambda i, j: (i, jnp.minimum(j, PROJ_TILES - 1))),
                   kv_spec, kv_spec,
                   pl.BlockSpec((tm, LANES), lambda i, j: (i, 0))],
        scratch_shapes=[pltpu.VMEM((tm, d), BF16)],
        input_output_aliases=aliases,
        compiler_params=_params("parallel", "arbitrary"),
        name="in_proj",
    )(*args)


def _chunk_mlp_kernel(u_ref, v_ref, ws_ref, b_ref, o_ref, *vn_refs, chunk_mask, nblk):
    row = lax.broadcasted_iota(jnp.int32, (CM_BLOCK, CM_BLOCK), 0)
    col = lax.broadcasted_iota(jnp.int32, (CM_BLOCK, CM_BLOCK), 1)
    visible = jnp.logical_or(row >= CHUNK, col < CHUNK)
    for g in range(CM_GROUPS):
        w = ws_ref[g]
        if chunk_mask:
            w = jnp.where(visible, w, 0.0)
        w = w.astype(BF16)
        bias = b_ref[g]
        cols = slice(g * CM_DIM, (g + 1) * CM_DIM)
        for n in range(nblk):
            rows = slice(n * CM_BLOCK, (n + 1) * CM_BLOCK)
            v = _gelu_tanh(v_ref[rows, cols])
            d = v - jnp.mean(v, axis=-1, keepdims=True)
            vn = d * lax.rsqrt(jnp.mean(d * d, axis=-1, keepdims=True) + EPS)
            if vn_refs:
                vn_refs[0][rows, cols] = vn
            s = _dot(w, vn.astype(BF16)) + bias
            o_ref[rows, cols] = (_gelu_tanh(u_ref[rows, cols]) * s).astype(BF16)


def _chunk_mlp(proj, ws, b, *, chunk_mask, want_vn, tm):
    n = proj.shape[0]
    nblk = tm // CM_BLOCK
    out_shape = [jax.ShapeDtypeStruct((n, CM_W), BF16)]
    out_specs = [pl.BlockSpec((tm, CM_W), lambda i: (i, 0))]
    if want_vn:
        out_shape.append(jax.ShapeDtypeStruct((n, CM_W), F32))
        out_specs.append(pl.BlockSpec((tm, CM_W), lambda i: (i, 0)))
    return pl.pallas_call(
        functools.partial(_chunk_mlp_kernel, chunk_mask=chunk_mask, nblk=nblk),
        out_shape=out_shape,
        grid=(n // tm,),
        in_specs=[pl.BlockSpec((tm, CM_W), lambda i: (i, COL_CM_U // CM_W)),
                  pl.BlockSpec((tm, CM_W), lambda i: (i, COL_CM_V // CM_W)),
                  pl.BlockSpec((CM_GROUPS, CM_BLOCK, CM_BLOCK), lambda i: (0, 0, 0)),
                  pl.BlockSpec((CM_GROUPS, CM_BLOCK, 1), lambda i: (0, 0, 0))],
        out_specs=out_specs,
        compiler_params=_params("parallel"),
        name="chunk_mlp",
    )(proj, proj, ws, b)


def _gla_kernel(q_ref, k_ref, v_ref, r_ref, glr_ref, w2_ref, b_ref, gn_ref, s0_ref,
                o_ref, st_ref, s_scr, *, chunk, nchunk):
    c = pl.program_id(1)

    @pl.when(c == 0)
    def _():
        s_scr[...] = s0_ref[0]

    row = lax.broadcasted_iota(jnp.int32, (chunk, chunk), 0)
    col = lax.broadcasted_iota(jnp.int32, (chunk, chunk), 1)
    causal = col <= row
    tril = jnp.where(causal, 1.0, 0.0).astype(BF16)

    def body(ci, carry):
        rows = pl.ds(pl.multiple_of(ci * chunk, chunk), chunk)
        x = _dot(glr_ref[rows, :].astype(BF16), w2_ref[...]) + b_ref[...]
        log_a = _log_sigmoid(x) * (1.0 / GLA_TAU)
        hi, lo = _split_bf16(log_a)
        cum = _dot(tril, hi) + _dot(tril, lo)
        last = cum[chunk - 1:chunk, :]
        heads = range(GLA_HEADS)
        qes, kes, kds, vs = [], [], [], []
        for h in heads:
            cs = _head_cols(h)
            cum_h = cum[:, cs]
            k = k_ref[rows, cs]
            qes.append(((q_ref[rows, cs] * (GLA_DK ** -0.5)) * jnp.exp(cum_h)).astype(BF16))
            kes.append((k * jnp.exp(-cum_h)).astype(BF16))
            kds.append((k * jnp.exp(last[:, cs] - cum_h)).astype(BF16))
            vs.append(v_ref[rows, cs].astype(BF16))
        scores = [_dot_nt(qes[h], kes[h]) for h in heads]
        states = [s_scr[h] for h in heads]
        carried = [_dot_nt(qes[h], states[h].astype(BF16)) for h in heads]
        updates = [_dot_tn(vs[h], kds[h]) for h in heads]
        atts = [jnp.where(causal, scores[h], 0.0).astype(BF16) for h in heads]
        for h in heads:
            cs = _head_cols(h)
            o = _dot(atts[h], vs[h]) + carried[h]
            s_scr[h] = states[h] * jnp.exp(last[:, cs]) + updates[h]
            on = _rms(o, gn_ref[:, cs])
            o_ref[rows, cs] = (on * _silu(r_ref[rows, cs])).astype(BF16)
        return carry

    lax.fori_loop(0, nchunk, body, 0)

    @pl.when(c == pl.num_programs(1) - 1)
    def _():
        st_ref[0] = s_scr[...]


def _gla(proj, glr, w2, b, gn, s0t, l, *, nseq, seq_len, chunk, rows_per_step):
    n = proj.shape[0]
    steps = seq_len // rows_per_step
    nchunk = rows_per_step // chunk
    w = GLA_V_W

    def rowmap(colblk):
        return lambda s, c: (s * steps + c, colblk)

    def layer(shape):
        return pl.BlockSpec((None,) + shape, lambda s, c: (l,) + (0,) * len(shape))

    return pl.pallas_call(
        functools.partial(_gla_kernel, chunk=chunk, nchunk=nchunk),
        out_shape=[jax.ShapeDtypeStruct((n, w), BF16),
                   jax.ShapeDtypeStruct((nseq, GLA_HEADS, LANES, LANES), F32)],
        grid=(nseq, steps),
        in_specs=[pl.BlockSpec((rows_per_step, w), rowmap(COL_GQ // w)),
                  pl.BlockSpec((rows_per_step, w), rowmap(COL_GK // w)),
                  pl.BlockSpec((rows_per_step, w), rowmap(COL_GV // w)),
                  pl.BlockSpec((rows_per_step, w), rowmap(COL_GR // w)),
                  pl.BlockSpec((rows_per_step, LANES), rowmap(0)),
                  layer((LANES, w)), layer((1, w)), layer((1, w)),
                  pl.BlockSpec((1, GLA_HEADS, LANES, LANES), lambda s, c: (s, 0, 0, 0))],
        out_specs=[pl.BlockSpec((rows_per_step, w), lambda s, c: (s * steps + c, 0)),
                   pl.BlockSpec((1, GLA_HEADS, LANES, LANES), lambda s, c: (s, 0, 0, 0))],
        scratch_shapes=[pltpu.VMEM((GLA_HEADS, LANES, LANES), F32)],
        compiler_params=_params("parallel", "arbitrary"),
        name="gla",
    )(proj, proj, proj, proj, glr, w2, b, gn, s0t)


def _sb_blocks(qns, k_ref, v_ref, carries, mask, uo):
    heads = range(SB_HEADS)
    zs = [_dot_nt(qns[h], k_ref[_head_rows(h, SB_BLK), :].astype(BF16)) * (SB_DIM ** -0.5) for h in heads]
    log_betas, splits = [], []
    for z in zs:
        log_beta = _log_sigmoid(z)
        log_stay = log_beta - z
        if mask is not None:
            log_stay = jnp.where(mask, log_stay, 0.0)
        log_betas.append(log_beta)
        splits.append(_split_bf16(log_stay))
    sums = [_dot(hi, uo) + _dot(lo, uo) for hi, lo in splits]
    weights = []
    for h in heads:
        later = sums[h][:, :SB_BLK]
        if carries is not None:
            later = later + carries[h]
        a = jnp.exp(log_betas[h] + later)
        if mask is not None:
            a = jnp.where(mask, a, 0.0)
        weights.append(a.astype(BF16))
    outs = [_dot(weights[h], v_ref[_head_rows(h, SB_BLK), :].astype(BF16)) for h in heads]
    return outs, [sums[h][:, SB_BLK:] for h in heads]


def _sb_diag(q_ref, gq_ref, kd_ref, vd_ref, uo, qn_scr, acc_scr, car_scr, tq):
    row = lax.broadcasted_iota(jnp.int32, (tq, SB_BLK), 0)
    col = lax.broadcasted_iota(jnp.int32, (tq, SB_BLK), 1)
    qns = [_rms(q_ref[:, _head_cols(h)], gq_ref[...]).astype(BF16) for h in range(SB_HEADS)]
    outs, tots = _sb_blocks(qns, kd_ref, vd_ref, None, col < row, uo)
    for h in range(SB_HEADS):
        cs = _head_cols(h)
        qn_scr[:, cs] = qns[h]
        acc_scr[:, cs] = outs[h]
        car_scr[:, cs] = tots[h]


def _sb_past(k_ref, v_ref, uo, qn_scr, acc_scr, car_scr):
    carries = [car_scr[:, _head_cols(h)] for h in range(SB_HEADS)]
    outs, tots = _sb_blocks([qn_scr[:, _head_cols(h)] for h in range(SB_HEADS)], k_ref, v_ref, carries, None, uo)
    for h in range(SB_HEADS):
        cs = _head_cols(h)
        acc_scr[:, cs] += outs[h]
        car_scr[:, cs] = carries[h] + tots[h]


def _sb_window_kernel(q_ref, gq_ref, kd_ref, vd_ref, k1_ref, v1_ref, k2_ref, v2_ref, uo_ref,
                      o_ref, mc_ref, qn_scr, acc_scr, car_scr, *, tq, n_past):
    npast = pl.program_id(1) if n_past is None else n_past
    uo = uo_ref[...]
    _sb_diag(q_ref, gq_ref, kd_ref, vd_ref, uo, qn_scr, acc_scr, car_scr, tq)
    for w, (k_ref, v_ref) in enumerate(((k1_ref, v1_ref), (k2_ref, v2_ref)), start=1):
        @pl.when(npast >= w)
        def _():
            _sb_past(k_ref, v_ref, uo, qn_scr, acc_scr, car_scr)

    o_ref[...] = acc_scr[...].astype(BF16)
    done = jnp.where(npast > SB_WINDOW, 0.0, SB_DONE)
    for h in range(SB_HEADS):
        mc_ref[0, h:h + 1, :] = jnp.max(car_scr[:, _head_cols(h)], axis=0, keepdims=True) + done


def _sb_window(proj, gq, kd, vd, kp, vp, uo, l, *, nseq, tq, qtiles, diag_blk0, past_blk, n_past):
    n = proj.shape[0]
    kv_blk = (SB_BLK * SB_HEADS, SB_DIM)
    past_specs = [pl.BlockSpec(kv_blk, functools.partial(lambda s, i, w: (past_blk(s, i, w), 0), w=w))
                  for w in (1, 1, 2, 2)]
    return pl.pallas_call(
        functools.partial(_sb_window_kernel, tq=tq, n_past=n_past),
        out_shape=[jax.ShapeDtypeStruct((n, SB_W), BF16),
                   jax.ShapeDtypeStruct((nseq * qtiles, SB_HEADS, LANES), F32)],
        grid=(nseq, qtiles),
        in_specs=[pl.BlockSpec((tq, SB_W), lambda s, i: (s * qtiles + i, COL_SQ // SB_W)),
                  pl.BlockSpec((None, 1, SB_DIM), lambda s, i: (l, 0, 0)),
                  pl.BlockSpec(kv_blk, lambda s, i: (diag_blk0 + s * qtiles + i, 0)),
                  pl.BlockSpec(kv_blk, lambda s, i: (diag_blk0 + s * qtiles + i, 0)),
                  *past_specs,
                  pl.BlockSpec((SB_BLK, 2 * SB_BLK), lambda s, i: (0, 0))],
        out_specs=[pl.BlockSpec((tq, SB_W), lambda s, i: (s * qtiles + i, 0)),
                   pl.BlockSpec((1, SB_HEADS, LANES), lambda s, i: (s * qtiles + i, 0, 0))],
        scratch_shapes=[pltpu.VMEM((tq, SB_W), BF16), pltpu.VMEM((tq, SB_W), F32), pltpu.VMEM((tq, SB_W), F32)],
        compiler_params=_params("parallel", "arbitrary"),
        name="sb_window",
    )(proj, gq, kd, vd, kp, vp, kp, vp, uo)


def _sb_full_kernel(flag_ref, q_ref, gq_ref, kd_ref, vd_ref, kp_ref, vp_ref, uo_ref, prev_ref,
                    o_ref, qn_scr, acc_scr, car_scr, live_scr, *, tq, qtiles, n_past):
    s, i, j = pl.program_id(0), pl.program_id(1), pl.program_id(2)
    npast = i if n_past is None else n_past
    flagged = flag_ref[s * qtiles + i] != 0

    def set_live():
        live_scr[0] = (jnp.max(car_scr[...]) > SB_LOG_ZERO).astype(jnp.int32)

    @pl.when(jnp.logical_and(flagged, j == 0))
    def _():
        _sb_diag(q_ref, gq_ref, kd_ref, vd_ref, uo_ref[...], qn_scr, acc_scr, car_scr, tq)
        set_live()

    @pl.when(jnp.logical_and(flagged, jnp.logical_and(j >= 1, j <= npast)))
    def _():
        @pl.when(live_scr[0] != 0)
        def _():
            _sb_past(kp_ref, vp_ref, uo_ref[...], qn_scr, acc_scr, car_scr)
            set_live()

    last = j == pl.num_programs(2) - 1

    @pl.when(jnp.logical_and(last, flagged))
    def _():
        o_ref[...] = acc_scr[...].astype(BF16)

    @pl.when(jnp.logical_and(last, jnp.logical_not(flagged)))
    def _():
        o_ref[...] = prev_ref[...]


def _sb_full(flags, proj, gq, kd, vd, kp, vp, uo, prev, l, *, nseq, tq, qtiles, diag_blk0, past_blk, n_past,
             max_past):
    n = proj.shape[0]
    kv_blk = (SB_BLK * SB_HEADS, SB_DIM)

    def past_map(s, i, j, flag_ref):
        w = jnp.clip(j, 1, max_past)
        return (jnp.where(flag_ref[s * qtiles + i] != 0, past_blk(s, i, w), past_blk(0, 0, 1)), 0)

    grid_spec = pltpu.PrefetchScalarGridSpec(
        num_scalar_prefetch=1,
        grid=(nseq, qtiles, max_past + 1),
        in_specs=[pl.BlockSpec((tq, SB_W), lambda s, i, j, f: (s * qtiles + i, COL_SQ // SB_W)),
                  pl.BlockSpec((None, 1, SB_DIM), lambda s, i, j, f: (l, 0, 0)),
                  pl.BlockSpec(kv_blk, lambda s, i, j, f: (diag_blk0 + s * qtiles + i, 0)),
                  pl.BlockSpec(kv_blk, lambda s, i, j, f: (diag_blk0 + s * qtiles + i, 0)),
                  pl.BlockSpec(kv_blk, past_map),
                  pl.BlockSpec(kv_blk, past_map),
                  pl.BlockSpec((SB_BLK, 2 * SB_BLK), lambda s, i, j, f: (0, 0)),
                  pl.BlockSpec((tq, SB_W), lambda s, i, j, f: (s * qtiles + i, 0))],
        out_specs=pl.BlockSpec((tq, SB_W), lambda s, i, j, f: (s * qtiles + i, 0)),
        scratch_shapes=[pltpu.VMEM((tq, SB_W), BF16), pltpu.VMEM((tq, SB_W), F32), pltpu.VMEM((tq, SB_W), F32),
                        pltpu.SMEM((1,), jnp.int32)])
    return pl.pallas_call(
        functools.partial(_sb_full_kernel, tq=tq, qtiles=qtiles, n_past=n_past),
        out_shape=jax.ShapeDtypeStruct((n, SB_W), BF16),
        grid_spec=grid_spec,
        compiler_params=_params("arbitrary", "arbitrary", "arbitrary"),
        name="sb_full",
    )(flags, proj, gq, kd, vd, kp, vp, uo, prev)


def _stick_breaking(proj, gq, kd, vd, kp, vp, uo, l, *, n_past, max_past, **geom):
    oc, carry = _sb_window(proj, gq, kd, vd, kp, vp, uo, l, n_past=n_past, **geom)
    flags = jnp.any(carry[:, :, 0] > SB_LOG_ZERO, axis=1).astype(jnp.int32)
    return lax.cond(jnp.any(flags != 0),
                    lambda: _sb_full(flags, proj, gq, kd, vd, kp, vp, uo, oc, l, n_past=n_past,
                                     max_past=max_past, **geom),
                    lambda: oc)


def _out_proj_kernel(x_ref, a_ref, b_ref, c_ref, w_ref, o_ref):
    o_ref[...] = (x_ref[...]
                  + _dot(a_ref[...], w_ref[0:CM_W, :])
                  + _dot(b_ref[...], w_ref[CM_W:CM_W + GLA_V_W, :])
                  + _dot(c_ref[...], w_ref[CM_W + GLA_V_W:, :]))


def _out_proj(x, oa, ob, oc, w, l, tm):
    n, d = x.shape
    return pl.pallas_call(
        _out_proj_kernel,
        out_shape=jax.ShapeDtypeStruct((n, d), F32),
        grid=(n // tm,),
        in_specs=[pl.BlockSpec((tm, d), lambda i: (i, 0)),
                  pl.BlockSpec((tm, CM_W), lambda i: (i, 0)),
                  pl.BlockSpec((tm, GLA_V_W), lambda i: (i, 0)),
                  pl.BlockSpec((tm, SB_W), lambda i: (i, 0)),
                  pl.BlockSpec((None,) + w.shape[1:], lambda i: (l, 0, 0))],
        out_specs=pl.BlockSpec((tm, d), lambda i: (i, 0)),
        compiler_params=_params("parallel"),
        name="out_proj",
    )(x, oa, ob, oc, w)


HALO = 16


def _ffn_conv(gate, prev1, prev2, cw_ref, cb_ref):
    return ((cb_ref[...] + cw_ref[0:1, :] * prev2) + cw_ref[1:2, :] * prev1) + cw_ref[2:3, :] * gate


def _ffn_weight_specs(l, d, tf, jmap):
    def spec(shape, blk):
        return pl.BlockSpec((None,) + shape, lambda *ids: (l,) + blk(jmap(*ids)))
    return [spec((1, d), lambda j: (0, 0)),
            spec((d, tf), lambda j: (0, j)),
            spec((d, tf), lambda j: (0, j)),
            spec((CONV_W, tf), lambda j: (0, j)),
            spec((1, tf), lambda j: (0, j)),
            spec((tf, d), lambda j: (j, 0))]


def _ffn_prompt_kernel(x_ref, xp_ref, g_ref, wg_ref, wu_ref, cw_ref, cb_ref, wd_ref,
                       o_ref, tail_ref, h_ref, *, tm, tiles_per_seq):
    i = pl.program_id(0)

    @pl.when(pl.program_id(1) == 0)
    def _():
        x = x_ref[...]
        h_ref[HALO:, :] = _rms(x, g_ref[...]).astype(BF16)
        h_ref[0:HALO, :] = _rms(xp_ref[...], g_ref[...]).astype(BF16)
        o_ref[...] = x

    h = h_ref[...]
    gate_all = _dot(h, wg_ref[...])
    halo_rows = jnp.where(i % tiles_per_seq == 0, HALO, 0)
    rowid = lax.broadcasted_iota(jnp.int32, gate_all.shape, 0)
    gate_all = jnp.where(rowid < halo_rows, 0.0, gate_all)
    prev1 = pltpu.roll(gate_all, 1, 0)[HALO:, :]
    prev2 = pltpu.roll(gate_all, 2, 0)[HALO:, :]
    gate = gate_all[HALO:, :]
    g = _ffn_conv(gate, prev1, prev2, cw_ref, cb_ref)
    up = _dot(h[HALO:, :], wu_ref[...])
    o_ref[...] += _dot((_silu(g) * up).astype(BF16), wd_ref[...])
    tail_ref[0] = gate[tm - 8:, :]


def _ffn_prompt(x, weights, l, *, seq_len, tm, tf):
    n, d = x.shape
    dff = weights[1].shape[-1]
    tiles_per_seq = seq_len // tm
    halo_blocks = tm // HALO
    return pl.pallas_call(
        functools.partial(_ffn_prompt_kernel, tm=tm, tiles_per_seq=tiles_per_seq),
        out_shape=[jax.ShapeDtypeStruct((n, d), F32),
                   jax.ShapeDtypeStruct((n // tm, 8, dff), F32)],
        grid=(n // tm, dff // tf),
        in_specs=[pl.BlockSpec((tm, d), lambda i, j: (i, 0)),
                  pl.BlockSpec((HALO, d), lambda i, j: (jnp.maximum(i * halo_blocks - 1, 0), 0)),
                  *_ffn_weight_specs(l, d, tf, lambda i, j: j)],
        out_specs=[pl.BlockSpec((tm, d), lambda i, j: (i, 0)),
                   pl.BlockSpec((1, 8, tf), lambda i, j: (i, 0, j))],
        scratch_shapes=[pltpu.VMEM((HALO + tm, d), BF16)],
        compiler_params=_params("parallel", "arbitrary"),
        name="ffn_prompt",
    )(x, x, *weights)


def _ffn_sample_kernel(x_ref, p1_ref, p2_ref, g_ref, wg_ref, wu_ref, cw_ref, cb_ref, wd_ref,
                       o_ref, gate_ref, h_ref, *, seq_len):
    @pl.when(pl.program_id(0) == 0)
    def _():
        x = x_ref[...]
        h_ref[...] = _rms(x, g_ref[...]).astype(BF16)
        o_ref[...] = x

    h = h_ref[...]
    gate = _dot(h, wg_ref[...])
    t = jnp.bitwise_and(lax.broadcasted_iota(jnp.int32, gate.shape, 0), seq_len - 1)
    prev1 = jnp.where(t >= 1, pltpu.roll(gate, 1, 0), p1_ref[...])
    prev2 = jnp.where(t >= 2, pltpu.roll(gate, 2, 0), p2_ref[...])
    g = _ffn_conv(gate, prev1, prev2, cw_ref, cb_ref)
    up = _dot(h, wu_ref[...])
    o_ref[...] += _dot((_silu(g) * up).astype(BF16), wd_ref[...])
    gate_ref[...] = gate


def _ffn_sample(x, p1, p2, weights, l, *, seq_len, tf):
    n, d = x.shape
    dff = weights[1].shape[-1]
    assert seq_len & (seq_len - 1) == 0
    return pl.pallas_call(
        functools.partial(_ffn_sample_kernel, seq_len=seq_len),
        out_shape=[jax.ShapeDtypeStruct((n, d), F32),
                   jax.ShapeDtypeStruct((n, dff), F32)],
        grid=(dff // tf,),
        in_specs=[pl.BlockSpec((n, d), lambda j: (0, 0)),
                  pl.BlockSpec((n, tf), lambda j: (0, j)),
                  pl.BlockSpec((n, tf), lambda j: (0, j)),
                  *_ffn_weight_specs(l, d, tf, lambda j: j)],
        out_specs=[pl.BlockSpec((n, d), lambda j: (0, 0)),
                   pl.BlockSpec((n, tf), lambda j: (0, j))],
        scratch_shapes=[pltpu.VMEM((n, d), BF16)],
        compiler_params=_params("arbitrary"),
        name="ffn_sample",
    )(x, p1, p2, *weights)


def _pack_w_in(w_in):
    nl, d, _ = w_in.shape
    o = 0
    seg = {}
    for name, width in (("cu", CM_W), ("cv", CM_W), ("gq", GLA_HEADS * GLA_DK), ("gk", GLA_HEADS * GLA_DK),
                        ("gv", GLA_V_W), ("gr", GLA_V_W), ("glr", GLA_RANK), ("sq", SB_W), ("sk", SB_W), ("sv", SB_W)):
        seg[name] = w_in[:, :, o:o + width]
        o += width

    def pad_heads(w):
        w = w.reshape(nl, d, GLA_HEADS, GLA_DK)
        return jnp.pad(w, ((0, 0), (0, 0), (0, 0), (0, LANES - GLA_DK))).reshape(nl, d, GLA_HEADS * LANES)

    w_all = jnp.concatenate([seg["cu"], seg["cv"], pad_heads(seg["gq"]), pad_heads(seg["gk"]),
                             seg["gv"], seg["gr"], seg["sq"], seg["sk"], seg["sv"]], axis=-1)
    w_glr = jnp.pad(seg["glr"], ((0, 0), (0, 0), (0, LANES - GLA_RANK)))
    return w_all.astype(BF16), w_glr.astype(BF16)


def _pack_gla_gate(gla_w2, gla_b):
    nl = gla_w2.shape[0]
    w2 = gla_w2.reshape(nl, GLA_RANK, GLA_HEADS, GLA_DK)
    w2 = jnp.pad(w2, ((0, 0), (0, LANES - GLA_RANK), (0, 0), (0, LANES - GLA_DK)))
    b = jnp.pad(gla_b.reshape(nl, 1, GLA_HEADS, GLA_DK), ((0, 0), (0, 0), (0, 0), (0, LANES - GLA_DK)))
    return w2.reshape(nl, LANES, GLA_HEADS * LANES).astype(BF16), b.reshape(nl, 1, GLA_HEADS * LANES)


def _state_to_kernel(s):
    st = jnp.swapaxes(s, 2, 3)
    return jnp.pad(st, ((0, 0), (0, 0), (0, 0), (0, LANES - GLA_DK)))


def _state_from_kernel(st):
    return jnp.swapaxes(st, 2, 3)[:, :, :GLA_DK, :]


def _later_sum_matrix():
    j = lax.broadcasted_iota(jnp.int32, (SB_BLK, 2 * SB_BLK), 0)
    s = lax.broadcasted_iota(jnp.int32, (SB_BLK, 2 * SB_BLK), 1)
    return jnp.where(jnp.logical_or(j > s, s >= SB_BLK), 1.0, 0.0).astype(BF16)


def _layer_prompt(x, lw, l, kv_bufs, *, nseq, seq_len, depth):
    n = x.shape[0]
    proj, k_all, v_all, glr = _in_proj(x, lw["norm1_g"], lw["w_all"], lw["w_glr"], lw["sb_k_g"], l, TM,
                                       kv_rows=depth * n, kv_row0=l * n, kv_bufs=kv_bufs)
    (oa,) = _chunk_mlp(proj, lw["cm_ws"][l], lw["cm_b"][l], chunk_mask=True, want_vn=False, tm=TM)
    s0t = jnp.zeros((nseq, GLA_HEADS, LANES, LANES), F32)
    ob, st = _gla(proj, glr, lw["gla_w2"], lw["gla_b"], lw["gla_gn"], s0t, l,
                  nseq=nseq, seq_len=seq_len, chunk=CHUNK, rows_per_step=TM)
    qtiles = seq_len // SB_BLK
    blk0 = l * n // SB_BLK

    def past_blk(s, i, w):
        return blk0 + s * qtiles + jnp.maximum(i - w, 0)

    oc = _stick_breaking(proj, lw["sb_q_g"], k_all, v_all, k_all, v_all, lw["uo"], l, nseq=nseq, tq=SB_BLK,
                         qtiles=qtiles, diag_blk0=blk0, past_blk=past_blk, n_past=None, max_past=qtiles - 1)
    x = _out_proj(x, oa, ob, oc, lw["w_out"], l, TM)
    x, tail = _ffn_prompt(x, lw["ffn"], l, seq_len=seq_len, tm=TM, tf=TF)
    tiles_per_seq = seq_len // TM
    conv_state = tail[tiles_per_seq - 1::tiles_per_seq, 8 - (CONV_W - 1):, :]
    return x, (k_all, v_all), _state_from_kernel(st), conv_state


def _layer_sample(x, lw, l, cache_k, cache_v, gla_s0, conv_prev, *, nseq, seq_len, past_len):
    n = x.shape[0]
    proj, k, v, glr = _in_proj(x, lw["norm1_g"], lw["w_all"], lw["w_glr"], lw["sb_k_g"], l, n,
                               kv_rows=n, kv_row0=0, kv_bufs=None)
    ws = lw["cm_ws"][l][:, :seq_len, :seq_len]
    ws_bd = jnp.einsum("ab,gij->gaibj", jnp.eye(nseq, dtype=F32), ws).reshape(CM_GROUPS, n, n)
    b_bd = jnp.tile(lw["cm_b"][l][:, :seq_len, :], (1, nseq, 1))
    oa, cm_v = _chunk_mlp(proj, ws_bd, b_bd, chunk_mask=False, want_vn=True, tm=n)
    ob, st = _gla(proj, glr, lw["gla_w2"], lw["gla_b"], lw["gla_gn"], _state_to_kernel(gla_s0), l,
                  nseq=nseq, seq_len=seq_len, chunk=seq_len, rows_per_step=seq_len)

    def pad_new(a):
        a = jnp.pad(a.reshape(nseq, seq_len, SB_HEADS, SB_DIM), ((0, 0), (0, SB_BLK - seq_len), (0, 0), (0, 0)))
        return a.reshape(nseq * SB_BLK * SB_HEADS, SB_DIM)

    past_blocks = past_len // SB_BLK

    def past_blk(s, i, w):
        return (l * nseq + s) * past_blocks + past_blocks - w

    oc = _stick_breaking(proj, lw["sb_q_g"], pad_new(k), pad_new(v), cache_k, cache_v, lw["uo"], l, nseq=nseq,
                         tq=seq_len, qtiles=1, diag_blk0=0, past_blk=past_blk, n_past=past_blocks,
                         max_past=past_blocks)
    x = _out_proj(x, oa, ob, oc, lw["w_out"], l, n)
    dff = conv_prev.shape[-1]
    p = jnp.zeros((nseq, seq_len, dff), F32)
    p1 = p.at[:, 0].set(conv_prev[:, 1]).reshape(n, dff)
    p2 = p.at[:, 0].set(conv_prev[:, 0]).at[:, 1].set(conv_prev[:, 1]).reshape(n, dff)
    x, gate = _ffn_sample(x, p1, p2, lw["ffn"], l, seq_len=seq_len, tf=TF)
    conv_state = gate.reshape(nseq, seq_len, dff)[:, seq_len - (CONV_W - 1):, :]
    return x, k, v, _state_from_kernel(st), conv_state, cm_v


def kernel(x_prompt, x_sample, cache_sb_k, cache_sb_v, state_gla, state_ffn_conv, norm1_g, w_in, cm_ws, cm_b,
           gla_w2, gla_b, gla_norm_g, sb_q_g, sb_k_g, w_out, norm2_g, ffn_w_gate, ffn_w_up, ffn_conv_w,
           ffn_conv_b, ffn_w_down):
    depth = w_in.shape[0]
    bp, tp, d = x_prompt.shape
    bs, ts, _ = x_sample.shape
    past_len = cache_sb_k.shape[2]
    assert d == D_MODEL and bs * ts == CM_BLOCK and tp % TM == 0 and past_len % SB_BLK == 0

    w_all, w_glr = _pack_w_in(w_in)
    w2, b2 = _pack_gla_gate(gla_w2, gla_b)
    lw = {
        "norm1_g": norm1_g[:, None, :], "w_all": w_all, "w_glr": w_glr,
        "cm_ws": cm_ws, "cm_b": cm_b[..., None],
        "gla_w2": w2, "gla_b": b2, "gla_gn": gla_norm_g.reshape(depth, 1, GLA_V_W),
        "sb_q_g": sb_q_g[:, None, :], "sb_k_g": sb_k_g[:, None, :], "uo": _later_sum_matrix(),
        "w_out": w_out.astype(BF16),
        "ffn": (norm2_g[:, None, :], ffn_w_gate.astype(BF16), ffn_w_up.astype(BF16), ffn_conv_w,
                ffn_conv_b[:, None, :], ffn_w_down.astype(BF16)),
    }
    cache_k = cache_sb_k.reshape(depth * bs * past_len * SB_HEADS, SB_DIM)
    cache_v = cache_sb_v.reshape(depth * bs * past_len * SB_HEADS, SB_DIM)

    yp = x_prompt.reshape(bp * tp, d)
    ys = x_sample.reshape(bs * ts, d)
    kv_p = None
    outs_p, outs_s = [], []
    for l in range(depth):
        yp, kv_p, sp, cp = _layer_prompt(yp, lw, l, kv_p, nseq=bp, seq_len=tp, depth=depth)
        ys, ks, vs, ss, cs, cmv = _layer_sample(ys, lw, l, cache_k, cache_v, state_gla[l], state_ffn_conv[l],
                                                nseq=bs, seq_len=ts, past_len=past_len)
        outs_p.append((sp, cp))
        outs_s.append((ks.reshape(bs, ts, SB_HEADS, SB_DIM), vs.reshape(bs, ts, SB_HEADS, SB_DIM), ss, cs,
                       cmv.reshape(bs, ts, CM_GROUPS, CM_DIM)))

    stack = lambda outs, i: jnp.stack([o[i] for o in outs])
    kv_shape = (depth, bp, tp, SB_HEADS, SB_DIM)
    return (yp.reshape(bp, tp, d), ys.reshape(bs, ts, d),
            kv_p[0].reshape(kv_shape), kv_p[1].reshape(kv_shape), stack(outs_p, 0), stack(outs_p, 1),
            stack(outs_s, 0), stack(outs_s, 1), stack(outs_s, 2), stack(outs_s, 3), stack(outs_s, 4))
```

```python
import functools

import jax
import jax.numpy as jnp
from jax import lax
from jax.experimental import pallas as pl
from jax.experimental.pallas import tpu as pltpu

F32 = jnp.float32
BF16 = jnp.bfloat16

D_MODEL = 2048
CHUNK = 64
CM_BLOCK = 128
CM_GROUPS = 4
CM_DIM = 128
CM_W = CM_GROUPS * CM_DIM
GLA_HEADS = 4
GLA_DK = 64
GLA_DV = 128
GLA_RANK = 16
GLA_TAU = 16.0
GLA_V_W = GLA_HEADS * GLA_DV
SB_HEADS = 8
SB_DIM = 128
SB_W = SB_HEADS * SB_DIM
D_FF = 5632
CONV_W = 3
EPS = 1e-6

LANES = 128
V7X_VMEM_LIMIT = 56 * 2 ** 20

TM = 512
TN = 2048
TF = 512

COL_CM_U = 0
COL_CM_V = 512
COL_GQ = 1024
COL_GK = 1536
COL_GV = 2048
COL_GR = 2560
COL_SQ = 3072
PROJ_W = 4096
PROJ_TILES = PROJ_W // TN
KV_TILES = 1
assert 2 * SB_W == TN

SB_BLK = 128
SB_WINDOW = 2
SB_LOG_ZERO = -105.0
SB_DONE = -1e30


def _dot(a, b):
    return jnp.dot(a, b, preferred_element_type=F32)


def _dot_nt(a, b):
    return lax.dot_general(a, b, (((1,), (1,)), ((), ())), preferred_element_type=F32)


def _dot_tn(a, b):
    return lax.dot_general(a, b, (((0,), (0,)), ((), ())), preferred_element_type=F32)


def _rms(x, g):
    ms = jnp.mean(x * x, axis=-1, keepdims=True)
    return x * lax.rsqrt(ms + EPS) * g


def _gelu_tanh(x):
    return x * (0.5 * (1.0 + jnp.tanh(0.7978845608028654 * (x + 0.044715 * (x * x * x)))))


def _log_sigmoid(x):
    return jnp.minimum(x, 0.0) - jnp.log(1.0 + jnp.exp(-jnp.abs(x)))


def _silu(x):
    return x * (1.0 / (1.0 + jnp.exp(-x)))


def _split_bf16(x):
    hi = x.astype(BF16)
    lo = (x - hi.astype(F32)).astype(BF16)
    return hi, lo


def _params(*sem):
    return pltpu.CompilerParams(dimension_semantics=sem, vmem_limit_bytes=V7X_VMEM_LIMIT)


def _head_cols(h):
    return slice(h * LANES, (h + 1) * LANES)


def _head_rows(h, ntok):
    return pl.ds(h, ntok, stride=SB_HEADS)


def _in_proj_kernel(*refs, n_alias):
    x_ref, g_ref, w_ref, wl_ref, gk_ref = refs[:5]
    proj_ref, k_ref, v_ref, glr_ref, h_ref = refs[5 + n_alias:]
    j = pl.program_id(1)

    @pl.when(j == 0)
    def _():
        h = _rms(x_ref[...], g_ref[...]).astype(BF16)
        h_ref[...] = h
        glr_ref[...] = _dot(h, wl_ref[...])

    @pl.when(j < PROJ_TILES)
    def _():
        proj_ref[...] = _dot(h_ref[...], w_ref[...])

    @pl.when(j == PROJ_TILES)
    def _():
        kv = _dot(h_ref[...], w_ref[...])
        rows = kv.shape[0]
        for h in range(SB_HEADS):
            k_ref[_head_rows(h, rows), :] = _rms(kv[:, _head_cols(h)], gk_ref[...])
        for h in range(SB_HEADS):
            v_ref[_head_rows(h, rows), :] = kv[:, _head_cols(SB_HEADS + h)]


def _in_proj(x, g, w_all, w_glr, gk, l, tm, *, kv_rows, kv_row0, kv_bufs):
    n, d = x.shape
    blk0 = kv_row0 // tm
    in_specs = [pl.BlockSpec((tm, d), lambda i, j: (i, 0)),
                pl.BlockSpec((None, 1, d), lambda i, j: (l, 0, 0)),
                pl.BlockSpec((None, d, TN), lambda i, j: (l, 0, j)),
                pl.BlockSpec((None, d, LANES), lambda i, j: (l, 0, 0)),
                pl.BlockSpec((None, 1, SB_DIM), lambda i, j: (l, 0, 0))]
    args = [x, g, w_all, w_glr, gk]
    aliases = {}
    if kv_bufs is not None:
        in_specs += [pl.BlockSpec(memory_space=pl.ANY)] * 2
        args += list(kv_bufs)
        aliases = {5: 1, 6: 2}
    kv_shape = jax.ShapeDtypeStruct((kv_rows * SB_HEADS, SB_DIM), F32)
    kv_spec = pl.BlockSpec((tm * SB_HEADS, SB_DIM), lambda i, j: (blk0 + i, 0))
    return pl.pallas_call(
        functools.partial(_in_proj_kernel, n_alias=len(aliases)),
        out_shape=[jax.ShapeDtypeStruct((n, PROJ_W), F32), kv_shape, kv_shape,
                   jax.ShapeDtypeStruct((n, LANES), F32)],
        grid=(n // tm, PROJ_TILES + KV_TILES),
        in_specs=in_specs,
        out_specs=[pl.BlockSpec((tm, TN), lambda i, j: (i, jnp.minimum(j, PROJ_TILES - 1))),
                   kv_spec, kv_spec,
                   pl.BlockSpec((tm, LANES), lambda i, j: (i, 0))],
        scratch_shapes=[pltpu.VMEM((tm, d), BF16)],
        input_output_aliases=aliases,
        compiler_params=_params("parallel", "arbitrary"),
        name="in_proj",
    )(*args)


def _chunk_mlp_kernel(u_ref, v_ref, ws_ref, b_ref, o_ref, *vn_refs, chunk_mask, nblk):
    row = lax.broadcasted_iota(jnp.int32, (CM_BLOCK, CM_BLOCK), 0)
    col = lax.broadcasted_iota(jnp.int32, (CM_BLOCK, CM_BLOCK), 1)
    visible = jnp.logical_or(row >= CHUNK, col < CHUNK)
    for g in range(CM_GROUPS):
        w = ws_ref[g]
        if chunk_mask:
            w = jnp.where(visible, w, 0.0)
        w = w.astype(BF16)
        bias = b_ref[g]
        cols = slice(g * CM_DIM, (g + 1) * CM_DIM)
        for n in range(nblk):
            rows = slice(n * CM_BLOCK, (n + 1) * CM_BLOCK)
            v = _gelu_tanh(v_ref[rows, cols])
            d = v - jnp.mean(v, axis=-1, keepdims=True)
            vn = d * lax.rsqrt(jnp.mean(d * d, axis=-1, keepdims=True) + EPS)
            if vn_refs:
                vn_refs[0][rows, cols] = vn
            s = _dot(w, vn.astype(BF16)) + bias
            o_ref[rows, cols] = (_gelu_tanh(u_ref[rows, cols]) * s).astype(BF16)


def _chunk_mlp(proj, ws, b, *, chunk_mask, want_vn, tm):
    n = proj.shape[0]
    nblk = tm // CM_BLOCK
    out_shape = [jax.ShapeDtypeStruct((n, CM_W), BF16)]
    out_specs = [pl.BlockSpec((tm, CM_W), lambda i: (i, 0))]
    if want_vn:
        out_shape.append(jax.ShapeDtypeStruct((n, CM_W), F32))
        out_specs.append(pl.BlockSpec((tm, CM_W), lambda i: (i, 0)))
    return pl.pallas_call(
        functools.partial(_chunk_mlp_kernel, chunk_mask=chunk_mask, nblk=nblk),
        out_shape=out_shape,
        grid=(n // tm,),
        in_specs=[pl.BlockSpec((tm, CM_W), lambda i: (i, COL_CM_U // CM_W)),
                  pl.BlockSpec((tm, CM_W), lambda i: (i, COL_CM_V // CM_W)),
                  pl.BlockSpec((CM_GROUPS, CM_BLOCK, CM_BLOCK), lambda i: (0, 0, 0)),
                  pl.BlockSpec((CM_GROUPS, CM_BLOCK, 1), lambda i: (0, 0, 0))],
        out_specs=out_specs,
        compiler_params=_params("parallel"),
        name="chunk_mlp",
    )(proj, proj, ws, b)


def _gla_kernel(q_ref, k_ref, v_ref, r_ref, glr_ref, w2_ref, b_ref, gn_ref, s0_ref,
                o_ref, st_ref, s_scr, *, chunk, nchunk):
    c = pl.program_id(1)

    @pl.when(c == 0)
    def _():
        s_scr[...] = s0_ref[0]

    row = lax.broadcasted_iota(jnp.int32, (chunk, chunk), 0)
    col = lax.broadcasted_iota(jnp.int32, (chunk, chunk), 1)
    causal = col <= row
    tril = jnp.where(causal, 1.0, 0.0).astype(BF16)

    heads = range(GLA_HEADS)
    chunks = range(nchunk)
    x = _dot(glr_ref[...].astype(BF16), w2_ref[...]) + b_ref[...]
    hi, lo = _split_bf16(_log_sigmoid(x) * (1.0 / GLA_TAU))
    qes, kes, kds, vs, decays = {}, {}, {}, {}, {}
    for ci in chunks:
        rows = slice(ci * chunk, (ci + 1) * chunk)
        cum = _dot(tril, hi[rows, :]) + _dot(tril, lo[rows, :])
        last = cum[chunk - 1:chunk, :]
        for h in heads:
            cs = _head_cols(h)
            cum_h = cum[:, cs]
            k = k_ref[rows, cs]
            qes[ci, h] = ((q_ref[rows, cs] * (GLA_DK ** -0.5)) * jnp.exp(cum_h)).astype(BF16)
            kes[ci, h] = (k * jnp.exp(-cum_h)).astype(BF16)
            kds[ci, h] = (k * jnp.exp(last[:, cs] - cum_h)).astype(BF16)
            vs[ci, h] = v_ref[rows, cs].astype(BF16)
            decays[ci, h] = jnp.exp(last[:, cs])
    pairs = [(ci, h) for ci in chunks for h in heads]
    scores = {p: _dot_nt(qes[p], kes[p]) for p in pairs}
    updates = {p: _dot_tn(vs[p], kds[p]) for p in pairs}
    atts = {p: jnp.where(causal, scores[p], 0.0).astype(BF16) for p in pairs}
    states = [s_scr[h] for h in heads]
    for ci in chunks:
        rows = slice(ci * chunk, (ci + 1) * chunk)
        carried = [_dot_nt(qes[ci, h], states[h].astype(BF16)) for h in heads]
        local = [_dot(atts[ci, h], vs[ci, h]) for h in heads]
        for h in heads:
            cs = _head_cols(h)
            states[h] = states[h] * decays[ci, h] + updates[ci, h]
            on = _rms(local[h] + carried[h], gn_ref[:, cs])
            o_ref[rows, cs] = (on * _silu(r_ref[rows, cs])).astype(BF16)
    for h in heads:
        s_scr[h] = states[h]

    @pl.when(c == pl.num_programs(1) - 1)
    def _():
        st_ref[0] = s_scr[...]


def _gla(proj, glr, w2, b, gn, s0t, l, *, nseq, seq_len, chunk, rows_per_step):
    n = proj.shape[0]
    steps = seq_len // rows_per_step
    nchunk = rows_per_step // chunk
    w = GLA_V_W

    def rowmap(colblk):
        return lambda s, c: (s * steps + c, colblk)

    def layer(shape):
        return pl.BlockSpec((None,) + shape, lambda s, c: (l,) + (0,) * len(shape))

    return pl.pallas_call(
        functools.partial(_gla_kernel, chunk=chunk, nchunk=nchunk),
        out_shape=[jax.ShapeDtypeStruct((n, w), BF16),
                   jax.ShapeDtypeStruct((nseq, GLA_HEADS, LANES, LANES), F32)],
        grid=(nseq, steps),
        in_specs=[pl.BlockSpec((rows_per_step, w), rowmap(COL_GQ // w)),
                  pl.BlockSpec((rows_per_step, w), rowmap(COL_GK // w)),
                  pl.BlockSpec((rows_per_step, w), rowmap(COL_GV // w)),
                  pl.BlockSpec((rows_per_step, w), rowmap(COL_GR // w)),
                  pl.BlockSpec((rows_per_step, LANES), rowmap(0)),
                  layer((LANES, w)), layer((1, w)), layer((1, w)),
                  pl.BlockSpec((1, GLA_HEADS, LANES, LANES), lambda s, c: (s, 0, 0, 0))],
        out_specs=[pl.BlockSpec((rows_per_step, w), lambda s, c: (s * steps + c, 0)),
                   pl.BlockSpec((1, GLA_HEADS, LANES, LANES), lambda s, c: (s, 0, 0, 0))],
        scratch_shapes=[pltpu.VMEM((GLA_HEADS, LANES, LANES), F32)],
        compiler_params=_params("parallel", "arbitrary"),
        name="gla",
    )(proj, proj, proj, proj, glr, w2, b, gn, s0t)


def _sb_sweep(qns, blocks, carries, uo):
    heads = range(SB_HEADS)
    pairs = [(b, h) for b in range(len(blocks)) for h in heads]
    zs = {(b, h): _dot_nt(qns[h], blocks[b][0][_head_rows(h, SB_BLK), :].astype(BF16)) * (SB_DIM ** -0.5)
          for b, h in pairs}
    log_betas, splits = {}, {}
    for b, h in pairs:
        mask = blocks[b][2]
        log_beta = _log_sigmoid(zs[b, h])
        log_stay = log_beta - zs[b, h]
        if mask is not None:
            log_stay = jnp.where(mask, log_stay, 0.0)
        log_betas[b, h] = log_beta
        splits[b, h] = _split_bf16(log_stay)
    sums = {p: _dot(splits[p][0], uo) + _dot(splits[p][1], uo) for p in pairs}
    carries = list(carries)
    weights = {}
    for b, h in pairs:
        mask = blocks[b][2]
        later = sums[b, h][:, :SB_BLK]
        total = sums[b, h][:, SB_BLK:]
        if carries[h] is not None:
            later = later + carries[h]
            total = total + carries[h]
        carries[h] = total
        a = jnp.exp(log_betas[b, h] + later)
        if mask is not None:
            a = jnp.where(mask, a, 0.0)
        weights[b, h] = a.astype(BF16)
    outs = []
    for h in heads:
        parts = [_dot(weights[b, h], blocks[b][1][_head_rows(h, SB_BLK), :].astype(BF16))
                 for b in range(len(blocks))]
        outs.append(functools.reduce(lambda x, y: x + y, parts))
    return outs, carries


def _sb_queries(q_ref, gq_ref):
    return [_rms(q_ref[:, _head_cols(h)], gq_ref[...]).astype(BF16) for h in range(SB_HEADS)]


def _sb_diag_mask(tq):
    row = lax.broadcasted_iota(jnp.int32, (tq, SB_BLK), 0)
    col = lax.broadcasted_iota(jnp.int32, (tq, SB_BLK), 1)
    return col < row


def _sb_window_kernel(q_ref, gq_ref, kd_ref, vd_ref, k1_ref, v1_ref, k2_ref, v2_ref, uo_ref,
                      o_ref, mc_ref, *, tq, n_past):
    npast = pl.program_id(1) if n_past is None else n_past
    window = [(kd_ref, vd_ref, _sb_diag_mask(tq)), (k1_ref, v1_ref, None), (k2_ref, v2_ref, None)]

    def run(nblocks):
        outs, carries = _sb_sweep(_sb_queries(q_ref, gq_ref), window[:nblocks], [None] * SB_HEADS, uo_ref[...])
        done = jnp.where(npast > SB_WINDOW, 0.0, SB_DONE)
        for h in range(SB_HEADS):
            o_ref[:, _head_cols(h)] = outs[h].astype(BF16)
            mc_ref[0, h:h + 1, :] = jnp.max(carries[h], axis=0, keepdims=True) + done

    if n_past is not None:
        run(1 + min(n_past, SB_WINDOW))
    else:
        for nb in range(1, SB_WINDOW + 1):
            pl.when(npast == nb - 1)(functools.partial(run, nb))
        pl.when(npast >= SB_WINDOW)(functools.partial(run, 1 + SB_WINDOW))


def _sb_window(proj, gq, kd, vd, kp, vp, uo, l, *, nseq, tq, qtiles, diag_blk0, past_blk, n_past):
    n = proj.shape[0]
    kv_blk = (SB_BLK * SB_HEADS, SB_DIM)
    past_specs = [pl.BlockSpec(kv_blk, functools.partial(lambda s, i, w: (past_blk(s, i, w), 0), w=w))
                  for w in (1, 1, 2, 2)]
    return pl.pallas_call(
        functools.partial(_sb_window_kernel, tq=tq, n_past=n_past),
        out_shape=[jax.ShapeDtypeStruct((n, SB_W), BF16),
                   jax.ShapeDtypeStruct((nseq * qtiles, SB_HEADS, LANES), F32)],
        grid=(nseq, qtiles),
        in_specs=[pl.BlockSpec((tq, SB_W), lambda s, i: (s * qtiles + i, COL_SQ // SB_W)),
                  pl.BlockSpec((None, 1, SB_DIM), lambda s, i: (l, 0, 0)),
                  pl.BlockSpec(kv_blk, lambda s, i: (diag_blk0 + s * qtiles + i, 0)),
                  pl.BlockSpec(kv_blk, lambda s, i: (diag_blk0 + s * qtiles + i, 0)),
                  *past_specs,
                  pl.BlockSpec((SB_BLK, 2 * SB_BLK), lambda s, i: (0, 0))],
        out_specs=[pl.BlockSpec((tq, SB_W), lambda s, i: (s * qtiles + i, 0)),
                   pl.BlockSpec((1, SB_HEADS, LANES), lambda s, i: (s * qtiles + i, 0, 0))],
        compiler_params=_params("parallel", "arbitrary"),
        name="sb_window",
    )(proj, gq, kd, vd, kp, vp, kp, vp, uo)


def _sb_full_kernel(flag_ref, q_ref, gq_ref, kd_ref, vd_ref, kp_ref, vp_ref, uo_ref, prev_ref,
                    o_ref, qn_scr, acc_scr, car_scr, live_scr, *, tq, qtiles, n_past):
    s, i, j = pl.program_id(0), pl.program_id(1), pl.program_id(2)
    npast = i if n_past is None else n_past
    flagged = flag_ref[s * qtiles + i] != 0

    heads = range(SB_HEADS)

    def keep(outs, carries, accumulate):
        for h in heads:
            cs = _head_cols(h)
            acc_scr[:, cs] = acc_scr[:, cs] + outs[h] if accumulate else outs[h]
            car_scr[:, cs] = carries[h]
        live_scr[0] = (jnp.max(car_scr[...]) > SB_LOG_ZERO).astype(jnp.int32)

    @pl.when(jnp.logical_and(flagged, j == 0))
    def _():
        qns = _sb_queries(q_ref, gq_ref)
        for h in heads:
            qn_scr[:, _head_cols(h)] = qns[h]
        keep(*_sb_sweep(qns, [(kd_ref, vd_ref, _sb_diag_mask(tq))], [None] * SB_HEADS, uo_ref[...]), False)

    @pl.when(jnp.logical_and(flagged, jnp.logical_and(j >= 1, j <= npast)))
    def _():
        @pl.when(live_scr[0] != 0)
        def _():
            qns = [qn_scr[:, _head_cols(h)] for h in heads]
            carries = [car_scr[:, _head_cols(h)] for h in heads]
            keep(*_sb_sweep(qns, [(kp_ref, vp_ref, None)], carries, uo_ref[...]), True)

    last = j == pl.num_programs(2) - 1

    @pl.when(jnp.logical_and(last, flagged))
    def _():
        o_ref[...] = acc_scr[...].astype(BF16)

    @pl.when(jnp.logical_and(last, jnp.logical_not(flagged)))
    def _():
        o_ref[...] = prev_ref[...]


def _sb_full(flags, proj, gq, kd, vd, kp, vp, uo, prev, l, *, nseq, tq, qtiles, diag_blk0, past_blk, n_past,
             max_past):
    n = proj.shape[0]
    kv_blk = (SB_BLK * SB_HEADS, SB_DIM)

    def past_map(s, i, j, flag_ref):
        w = jnp.clip(j, 1, max_past)
        return (jnp.where(flag_ref[s * qtiles + i] != 0, past_blk(s, i, w), past_blk(0, 0, 1)), 0)

    grid_spec = pltpu.PrefetchScalarGridSpec(
        num_scalar_prefetch=1,
        grid=(nseq, qtiles, max_past + 1),
        in_specs=[pl.BlockSpec((tq, SB_W), lambda s, i, j, f: (s * qtiles + i, COL_SQ // SB_W)),
                  pl.BlockSpec((None, 1, SB_DIM), lambda s, i, j, f: (l, 0, 0)),
                  pl.BlockSpec(kv_blk, lambda s, i, j, f: (diag_blk0 + s * qtiles + i, 0)),
                  pl.BlockSpec(kv_blk, lambda s, i, j, f: (diag_blk0 + s * qtiles + i, 0)),
                  pl.BlockSpec(kv_blk, past_map),
                  pl.BlockSpec(kv_blk, past_map),
                  pl.BlockSpec((SB_BLK, 2 * SB_BLK), lambda s, i, j, f: (0, 0)),
                  pl.BlockSpec((tq, SB_W), lambda s, i, j, f: (s * qtiles + i, 0))],
        out_specs=pl.BlockSpec((tq, SB_W), lambda s, i, j, f: (s * qtiles + i, 0)),
        scratch_shapes=[pltpu.VMEM((tq, SB_W), BF16), pltpu.VMEM((tq, SB_W), F32), pltpu.VMEM((tq, SB_W), F32),
                        pltpu.SMEM((1,), jnp.int32)])
    return pl.pallas_call(
        functools.partial(_sb_full_kernel, tq=tq, qtiles=qtiles, n_past=n_past),
        out_shape=jax.ShapeDtypeStruct((n, SB_W), BF16),
        grid_spec=grid_spec,
        compiler_params=_params("arbitrary", "arbitrary", "arbitrary"),
        name="sb_full",
    )(flags, proj, gq, kd, vd, kp, vp, uo, prev)


def _stick_breaking(proj, gq, kd, vd, kp, vp, uo, l, *, n_past, max_past, **geom):
    oc, carry = _sb_window(proj, gq, kd, vd, kp, vp, uo, l, n_past=n_past, **geom)
    flags = jnp.any(carry[:, :, 0] > SB_LOG_ZERO, axis=1).astype(jnp.int32)
    return lax.cond(jnp.any(flags != 0),
                    lambda: _sb_full(flags, proj, gq, kd, vd, kp, vp, uo, oc, l, n_past=n_past,
                                     max_past=max_past, **geom),
                    lambda: oc)


def _out_proj_kernel(x_ref, a_ref, b_ref, c_ref, w_ref, o_ref):
    o_ref[...] = (x_ref[...]
                  + _dot(a_ref[...], w_ref[0:CM_W, :])
                  + _dot(b_ref[...], w_ref[CM_W:CM_W + GLA_V_W, :])
                  + _dot(c_ref[...], w_ref[CM_W + GLA_V_W:, :]))


def _out_proj(x, oa, ob, oc, w, l, tm):
    n, d = x.shape
    return pl.pallas_call(
        _out_proj_kernel,
        out_shape=jax.ShapeDtypeStruct((n, d), F32),
        grid=(n // tm,),
        in_specs=[pl.BlockSpec((tm, d), lambda i: (i, 0)),
                  pl.BlockSpec((tm, CM_W), lambda i: (i, 0)),
                  pl.BlockSpec((tm, GLA_V_W), lambda i: (i, 0)),
                  pl.BlockSpec((tm, SB_W), lambda i: (i, 0)),
                  pl.BlockSpec((None,) + w.shape[1:], lambda i: (l, 0, 0))],
        out_specs=pl.BlockSpec((tm, d), lambda i: (i, 0)),
        compiler_params=_params("parallel"),
        name="out_proj",
    )(x, oa, ob, oc, w)


HALO = 16


def _ffn_conv(gate, prev1, prev2, cw_ref, cb_ref):
    return ((cb_ref[...] + cw_ref[0:1, :] * prev2) + cw_ref[1:2, :] * prev1) + cw_ref[2:3, :] * gate


def _ffn_weight_specs(l, d, tf, jmap):
    def spec(shape, blk):
        return pl.BlockSpec((None,) + shape, lambda *ids: (l,) + blk(jmap(*ids)))
    return [spec((1, d), lambda j: (0, 0)),
            spec((d, tf), lambda j: (0, j)),
            spec((d, tf), lambda j: (0, j)),
            spec((CONV_W, tf), lambda j: (0, j)),
            spec((1, tf), lambda j: (0, j)),
            spec((tf, d), lambda j: (j, 0))]


def _ffn_prompt_kernel(x_ref, xp_ref, g_ref, wg_ref, wu_ref, cw_ref, cb_ref, wd_ref,
                       o_ref, tail_ref, h_ref, *, tm, tiles_per_seq):
    i = pl.program_id(0)

    @pl.when(pl.program_id(1) == 0)
    def _():
        x = x_ref[...]
        h_ref[HALO:, :] = _rms(x, g_ref[...]).astype(BF16)
        h_ref[0:HALO, :] = _rms(xp_ref[...], g_ref[...]).astype(BF16)
        o_ref[...] = x

    h = h_ref[...]
    gate_all = _dot(h, wg_ref[...])
    halo_rows = jnp.where(i % tiles_per_seq == 0, HALO, 0)
    rowid = lax.broadcasted_iota(jnp.int32, gate_all.shape, 0)
    gate_all = jnp.where(rowid < halo_rows, 0.0, gate_all)
    prev1 = pltpu.roll(gate_all, 1, 0)[HALO:, :]
    prev2 = pltpu.roll(gate_all, 2, 0)[HALO:, :]
    gate = gate_all[HALO:, :]
    g = _ffn_conv(gate, prev1, prev2, cw_ref, cb_ref)
    up = _dot(h[HALO:, :], wu_ref[...])
    o_ref[...] += _dot((_silu(g) * up).astype(BF16), wd_ref[...])
    tail_ref[0] = gate[tm - 8:, :]


def _ffn_prompt(x, weights, l, *, seq_len, tm, tf):
    n, d = x.shape
    dff = weights[1].shape[-1]
    tiles_per_seq = seq_len // tm
    halo_blocks = tm // HALO
    return pl.pallas_call(
        functools.partial(_ffn_prompt_kernel, tm=tm, tiles_per_seq=tiles_per_seq),
        out_shape=[jax.ShapeDtypeStruct((n, d), F32),
                   jax.ShapeDtypeStruct((n // tm, 8, dff), F32)],
        grid=(n // tm, dff // tf),
        in_specs=[pl.BlockSpec((tm, d), lambda i, j: (i, 0)),
                  pl.BlockSpec((HALO, d), lambda i, j: (jnp.maximum(i * halo_blocks - 1, 0), 0)),
                  *_ffn_weight_specs(l, d, tf, lambda i, j: j)],
        out_specs=[pl.BlockSpec((tm, d), lambda i, j: (i, 0)),
                   pl.BlockSpec((1, 8, tf), lambda i, j: (i, 0, j))],
        scratch_shapes=[pltpu.VMEM((HALO + tm, d), BF16)],
        compiler_params=_params("parallel", "arbitrary"),
        name="ffn_prompt",
    )(x, x, *weights)


def _ffn_sample_kernel(x_ref, p1_ref, p2_ref, g_ref, wg_ref, wu_ref, cw_ref, cb_ref, wd_ref,
                       o_ref, gate_ref, h_ref, *, seq_len):
    @pl.when(pl.program_id(0) == 0)
    def _():
        x = x_ref[...]
        h_ref[...] = _rms(x, g_ref[...]).astype(BF16)
        o_ref[...] = x

    h = h_ref[...]
    gate = _dot(h, wg_ref[...])
    t = jnp.bitwise_and(lax.broadcasted_iota(jnp.int32, gate.shape, 0), seq_len - 1)
    prev1 = jnp.where(t >= 1, pltpu.roll(gate, 1, 0), p1_ref[...])
    prev2 = jnp.where(t >= 2, pltpu.roll(gate, 2, 0), p2_ref[...])
    g = _ffn_conv(gate, prev1, prev2, cw_ref, cb_ref)
    up = _dot(h, wu_ref[...])
    o_ref[...] += _dot((_silu(g) * up).astype(BF16), wd_ref[...])
    gate_ref[...] = gate


def _ffn_sample(x, p1, p2, weights, l, *, seq_len, tf):
    n, d = x.shape
    dff = weights[1].shape[-1]
    assert seq_len & (seq_len - 1) == 0
    return pl.pallas_call(
        functools.partial(_ffn_sample_kernel, seq_len=seq_len),
        out_shape=[jax.ShapeDtypeStruct((n, d), F32),
                   jax.ShapeDtypeStruct((n, dff), F32)],
        grid=(dff // tf,),
        in_specs=[pl.BlockSpec((n, d), lambda j: (0, 0)),
                  pl.BlockSpec((n, tf), lambda j: (0, j)),
                  pl.BlockSpec((n, tf), lambda j: (0, j)),
                  *_ffn_weight_specs(l, d, tf, lambda j: j)],
        out_specs=[pl.BlockSpec((n, d), lambda j: (0, 0)),
                   pl.BlockSpec((n, tf), lambda j: (0, j))],
        scratch_shapes=[pltpu.VMEM((n, d), BF16)],
        compiler_params=_params("arbitrary"),
        name="ffn_sample",
    )(x, p1, p2, *weights)


def _pack_w_in(w_in):
    nl, d, _ = w_in.shape
    o = 0
    seg = {}
    for name, width in (("cu", CM_W), ("cv", CM_W), ("gq", GLA_HEADS * GLA_DK), ("gk", GLA_HEADS * GLA_DK),
                        ("gv", GLA_V_W), ("gr", GLA_V_W), ("glr", GLA_RANK), ("sq", SB_W), ("sk", SB_W), ("sv", SB_W)):
        seg[name] = w_in[:, :, o:o + width]
        o += width

    def pad_heads(w):
        w = w.reshape(nl, d, GLA_HEADS, GLA_DK)
        return jnp.pad(w, ((0, 0), (0, 0), (0, 0), (0, LANES - GLA_DK))).reshape(nl, d, GLA_HEADS * LANES)

    w_all = jnp.concatenate([seg["cu"], seg["cv"], pad_heads(seg["gq"]), pad_heads(seg["gk"]),
                             seg["gv"], seg["gr"], seg["sq"], seg["sk"], seg["sv"]], axis=-1)
    w_glr = jnp.pad(seg["glr"], ((0, 0), (0, 0), (0, LANES - GLA_RANK)))
    return w_all.astype(BF16), w_glr.astype(BF16)


def _pack_gla_gate(gla_w2, gla_b):
    nl = gla_w2.shape[0]
    w2 = gla_w2.reshape(nl, GLA_RANK, GLA_HEADS, GLA_DK)
    w2 = jnp.pad(w2, ((0, 0), (0, LANES - GLA_RANK), (0, 0), (0, LANES - GLA_DK)))
    b = jnp.pad(gla_b.reshape(nl, 1, GLA_HEADS, GLA_DK), ((0, 0), (0, 0), (0, 0), (0, LANES - GLA_DK)))
    return w2.reshape(nl, LANES, GLA_HEADS * LANES).astype(BF16), b.reshape(nl, 1, GLA_HEADS * LANES)


def _state_to_kernel(s):
    st = jnp.swapaxes(s, 2, 3)
    return jnp.pad(st, ((0, 0), (0, 0), (0, 0), (0, LANES - GLA_DK)))


def _state_from_kernel(st):
    return jnp.swapaxes(st, 2, 3)[:, :, :GLA_DK, :]


def _later_sum_matrix():
    j = lax.broadcasted_iota(jnp.int32, (SB_BLK, 2 * SB_BLK), 0)
    s = lax.broadcasted_iota(jnp.int32, (SB_BLK, 2 * SB_BLK), 1)
    return jnp.where(jnp.logical_or(j > s, s >= SB_BLK), 1.0, 0.0).astype(BF16)


def _layer_prompt(x, lw, l, kv_bufs, *, nseq, seq_len, depth):
    n = x.shape[0]
    proj, k_all, v_all, glr = _in_proj(x, lw["norm1_g"], lw["w_all"], lw["w_glr"], lw["sb_k_g"], l, TM,
                                       kv_rows=depth * n, kv_row0=l * n, kv_bufs=kv_bufs)
    (oa,) = _chunk_mlp(proj, lw["cm_ws"][l], lw["cm_b"][l], chunk_mask=True, want_vn=False, tm=TM)
    s0t = jnp.zeros((nseq, GLA_HEADS, LANES, LANES), F32)
    ob, st = _gla(proj, glr, lw["gla_w2"], lw["gla_b"], lw["gla_gn"], s0t, l,
                  nseq=nseq, seq_len=seq_len, chunk=CHUNK, rows_per_step=TM)
    qtiles = seq_len // SB_BLK
    blk0 = l * n // SB_BLK

    def past_blk(s, i, w):
        return blk0 + s * qtiles + jnp.maximum(i - w, 0)

    oc = _stick_breaking(proj, lw["sb_q_g"], k_all, v_all, k_all, v_all, lw["uo"], l, nseq=nseq, tq=SB_BLK,
                         qtiles=qtiles, diag_blk0=blk0, past_blk=past_blk, n_past=None, max_past=qtiles - 1)
    x = _out_proj(x, oa, ob, oc, lw["w_out"], l, TM)
    x, tail = _ffn_prompt(x, lw["ffn"], l, seq_len=seq_len, tm=TM, tf=TF)
    tiles_per_seq = seq_len // TM
    conv_state = tail[tiles_per_seq - 1::tiles_per_seq, 8 - (CONV_W - 1):, :]
    return x, (k_all, v_all), _state_from_kernel(st), conv_state


def _layer_sample(x, lw, l, cache_k, cache_v, gla_s0, conv_prev, *, nseq, seq_len, past_len):
    n = x.shape[0]
    proj, k, v, glr = _in_proj(x, lw["norm1_g"], lw["w_all"], lw["w_glr"], lw["sb_k_g"], l, n,
                               kv_rows=n, kv_row0=0, kv_bufs=None)
    ws = lw["cm_ws"][l][:, :seq_len, :seq_len]
    ws_bd = jnp.einsum("ab,gij->gaibj", jnp.eye(nseq, dtype=F32), ws).reshape(CM_GROUPS, n, n)
    b_bd = jnp.tile(lw["cm_b"][l][:, :seq_len, :], (1, nseq, 1))
    oa, cm_v = _chunk_mlp(proj, ws_bd, b_bd, chunk_mask=False, want_vn=True, tm=n)
    ob, st = _gla(proj, glr, lw["gla_w2"], lw["gla_b"], lw["gla_gn"], _state_to_kernel(gla_s0), l,
                  nseq=nseq, seq_len=seq_len, chunk=seq_len, rows_per_step=seq_len)

    def pad_new(a):
        a = jnp.pad(a.reshape(nseq, seq_len, SB_HEADS, SB_DIM), ((0, 0), (0, SB_BLK - seq_len), (0, 0), (0, 0)))
        return a.reshape(nseq * SB_BLK * SB_HEADS, SB_DIM)

    past_blocks = past_len // SB_BLK

    def past_blk(s, i, w):
        return (l * nseq + s) * past_blocks + past_blocks - w

    oc = _stick_breaking(proj, lw["sb_q_g"], pad_new(k), pad_new(v), cache_k, cache_v, lw["uo"], l, nseq=nseq,
                         tq=seq_len, qtiles=1, diag_blk0=0, past_blk=past_blk, n_past=past_blocks,
                         max_past=past_blocks)
    x = _out_proj(x, oa, ob, oc, lw["w_out"], l, n)
    dff = conv_prev.shape[-1]
    p = jnp.zeros((nseq, seq_len, dff), F32)
    p1 = p.at[:, 0].set(conv_prev[:, 1]).reshape(n, dff)
    p2 = p.at[:, 0].set(conv_prev[:, 0]).at[:, 1].set(conv_prev[:, 1]).reshape(n, dff)
    x, gate = _ffn_sample(x, p1, p2, lw["ffn"], l, seq_len=seq_len, tf=TF)
    conv_state = gate.reshape(nseq, seq_len, dff)[:, seq_len - (CONV_W - 1):, :]
    return x, k, v, _state_from_kernel(st), conv_state, cm_v


def kernel(x_prompt, x_sample, cache_sb_k, cache_sb_v, state_gla, state_ffn_conv, norm1_g, w_in, cm_ws, cm_b,
           gla_w2, gla_b, gla_norm_g, sb_q_g, sb_k_g, w_out, norm2_g, ffn_w_gate, ffn_w_up, ffn_conv_w,
           ffn_conv_b, ffn_w_down):
    depth = w_in.shape[0]
    bp, tp, d = x_prompt.shape
    bs, ts, _ = x_sample.shape
    past_len = cache_sb_k.shape[2]
    assert d == D_MODEL and bs * ts == CM_BLOCK and tp % TM == 0 and past_len % SB_BLK == 0

    w_all, w_glr = _pack_w_in(w_in)
    w2, b2 = _pack_gla_gate(gla_w2, gla_b)
    lw = {
        "norm1_g": norm1_g[:, None, :], "w_all": w_all, "w_glr": w_glr,
        "cm_ws": cm_ws, "cm_b": cm_b[..., None],
        "gla_w2": w2, "gla_b": b2, "gla_gn": gla_norm_g.reshape(depth, 1, GLA_V_W),
        "sb_q_g": sb_q_g[:, None, :], "sb_k_g": sb_k_g[:, None, :], "uo": _later_sum_matrix(),
        "w_out": w_out.astype(BF16),
        "ffn": (norm2_g[:, None, :], ffn_w_gate.astype(BF16), ffn_w_up.astype(BF16), ffn_conv_w,
                ffn_conv_b[:, None, :], ffn_w_down.astype(BF16)),
    }
    cache_k = cache_sb_k.reshape(depth * bs * past_len * SB_HEADS, SB_DIM)
    cache_v = cache_sb_v.reshape(depth * bs * past_len * SB_HEADS, SB_DIM)

    yp = x_prompt.reshape(bp * tp, d)
    ys = x_sample.reshape(bs * ts, d)
    kv_p = None
    outs_p, outs_s = [], []
    for l in range(depth):
        yp, kv_p, sp, cp = _layer_prompt(yp, lw, l, kv_p, nseq=bp, seq_len=tp, depth=depth)
        ys, ks, vs, ss, cs, cmv = _layer_sample(ys, lw, l, cache_k, cache_v, state_gla[l], state_ffn_conv[l],
                                                nseq=bs, seq_len=ts, past_len=past_len)
        outs_p.append((sp, cp))
        outs_s.append((ks.reshape(bs, ts, SB_HEADS, SB_DIM), vs.reshape(bs, ts, SB_HEADS, SB_DIM), ss, cs,
                       cmv.reshape(bs, ts, CM_GROUPS, CM_DIM)))

    stack = lambda outs, i: jnp.stack([o[i] for o in outs])
    kv_shape = (depth, bp, tp, SB_HEADS, SB_DIM)
    return (yp.reshape(bp, tp, d), ys.reshape(bs, ts, d),
            kv_p[0].reshape(kv_shape), kv_p[1].reshape(kv_shape), stack(outs_p, 0), stack(outs_p, 1),
            stack(outs_s, 0), stack(outs_s, 1), stack(outs_s, 2), stack(outs_s, 3), stack(outs_s, 4))
```

```python
import functools

import jax
import jax.numpy as jnp
from jax import lax
from jax.experimental import pallas as pl
from jax.experimental.pallas import tpu as pltpu

F32 = jnp.float32
BF16 = jnp.bfloat16

D_MODEL = 2048
CHUNK = 64
CM_BLOCK = 128
CM_GROUPS = 4
CM_DIM = 128
CM_W = CM_GROUPS * CM_DIM
GLA_HEADS = 4
GLA_DK = 64
GLA_DV = 128
GLA_RANK = 16
GLA_TAU = 16.0
GLA_V_W = GLA_HEADS * GLA_DV
SB_HEADS = 8
SB_DIM = 128
SB_W = SB_HEADS * SB_DIM
D_FF = 5632
CONV_W = 3
EPS = 1e-6

LANES = 128
V7X_VMEM_LIMIT = 56 * 2 ** 20

TM = 512
TM_FFN = 1024
TN = 2048
TF = 512

COL_CM_U = 0
COL_CM_V = 512
COL_GQ = 1024
COL_GK = 1536
COL_GV = 2048
COL_GR = 2560
COL_SQ = 3072
PROJ_W = 4096
PROJ_TILES = PROJ_W // TN
KV_TILES = 1
assert 2 * SB_W == TN

SB_BLK = 128
SB_WINDOW = 2
SB_LAST_ROWS = 64
SB_LOG_ZERO = -105.0
SB_DONE = -1e30


def _dot(a, b):
    return jnp.dot(a, b, preferred_element_type=F32)


def _dot_nt(a, b):
    return lax.dot_general(a, b, (((1,), (1,)), ((), ())), preferred_element_type=F32)


def _dot_tn(a, b):
    return lax.dot_general(a, b, (((0,), (0,)), ((), ())), preferred_element_type=F32)


def _rms(x, g):
    ms = jnp.mean(x * x, axis=-1, keepdims=True)
    return x * lax.rsqrt(ms + EPS) * g


def _gelu_tanh(x):
    return x * (0.5 * (1.0 + jnp.tanh(0.7978845608028654 * (x + 0.044715 * (x * x * x)))))


def _log_sigmoid(x):
    return jnp.minimum(x, 0.0) - jnp.log(1.0 + jnp.exp(-jnp.abs(x)))


def _silu(x):
    return x * (1.0 / (1.0 + jnp.exp(-x)))


def _split_bf16(x):
    hi = x.astype(BF16)
    lo = (x - hi.astype(F32)).astype(BF16)
    return hi, lo


def _params(*sem):
    return pltpu.CompilerParams(dimension_semantics=sem, vmem_limit_bytes=V7X_VMEM_LIMIT)


def _head_cols(h):
    return slice(h * LANES, (h + 1) * LANES)


def _head_rows(h, ntok):
    return pl.ds(h, ntok, stride=SB_HEADS)


def _in_proj_kernel(*refs, n_alias):
    x_ref, g_ref, w_ref, wl_ref, gk_ref = refs[:5]
    proj_ref, k_ref, v_ref, glr_ref, h_ref = refs[5 + n_alias:]
    j = pl.program_id(1)

    @pl.when(j == 0)
    def _():
        h = _rms(x_ref[...], g_ref[...]).astype(BF16)
        h_ref[...] = h
        glr_ref[...] = _dot(h, wl_ref[...])

    @pl.when(j < PROJ_TILES)
    def _():
        proj_ref[...] = _dot(h_ref[...], w_ref[...])

    @pl.when(j == PROJ_TILES)
    def _():
        kv = _dot(h_ref[...], w_ref[...])
        rows = kv.shape[0]
        for h in range(SB_HEADS):
            k_ref[_head_rows(h, rows), :] = _rms(kv[:, _head_cols(h)], gk_ref[...])
        for h in range(SB_HEADS):
            v_ref[_head_rows(h, rows), :] = kv[:, _head_cols(SB_HEADS + h)]


def _in_proj(x, g, w_all, w_glr, gk, l, tm, *, kv_rows, kv_row0, kv_bufs):
    n, d = x.shape
    blk0 = kv_row0 // tm
    in_specs = [pl.BlockSpec((tm, d), lambda i, j: (i, 0)),
                pl.BlockSpec((None, 1, d), lambda i, j: (l, 0, 0)),
                pl.BlockSpec((None, d, TN), lambda i, j: (l, 0, j)),
                pl.BlockSpec((None, d, LANES), lambda i, j: (l, 0, 0)),
                pl.BlockSpec((None, 1, SB_DIM), lambda i, j: (l, 0, 0))]
    args = [x, g, w_all, w_glr, gk]
    aliases = {}
    if kv_bufs is not None:
        in_specs += [pl.BlockSpec(memory_space=pl.ANY)] * 2
        args += list(kv_bufs)
        aliases = {5: 1, 6: 2}
    kv_shape = jax.ShapeDtypeStruct((kv_rows * SB_HEADS, SB_DIM), F32)
    kv_spec = pl.BlockSpec((tm * SB_HEADS, SB_DIM), lambda i, j: (blk0 + i, 0))
    return pl.pallas_call(
        functools.partial(_in_proj_kernel, n_alias=len(aliases)),
        out_shape=[jax.ShapeDtypeStruct((n, PROJ_W), F32), kv_shape, kv_shape,
                   jax.ShapeDtypeStruct((n, LANES), F32)],
        grid=(n // tm, PROJ_TILES + KV_TILES),
        in_specs=in_specs,
        out_specs=[pl.BlockSpec((tm, TN), lambda i, j: (i, jnp.minimum(j, PROJ_TILES - 1))),
                   kv_spec, kv_spec,
                   pl.BlockSpec((tm, LANES), lambda i, j: (i, 0))],
        scratch_shapes=[pltpu.VMEM((tm, d), BF16)],
        input_output_aliases=aliases,
        compiler_params=_params("parallel", "arbitrary"),
        name="in_proj",
    )(*args)


def _chunk_mlp_kernel(u_ref, v_ref, ws_ref, b_ref, o_ref, *vn_refs, chunk_mask, nblk):
    row = lax.broadcasted_iota(jnp.int32, (CM_BLOCK, CM_BLOCK), 0)
    col = lax.broadcasted_iota(jnp.int32, (CM_BLOCK, CM_BLOCK), 1)
    visible = jnp.logical_or(row >= CHUNK, col < CHUNK)
    for g in range(CM_GROUPS):
        w = ws_ref[g]
        if chunk_mask:
            w = jnp.where(visible, w, 0.0)
        w = w.astype(BF16)
        bias = b_ref[g]
        cols = slice(g * CM_DIM, (g + 1) * CM_DIM)
        for n in range(nblk):
            rows = slice(n * CM_BLOCK, (n + 1) * CM_BLOCK)
            v = _gelu_tanh(v_ref[rows, cols])
            d = v - jnp.mean(v, axis=-1, keepdims=True)
            vn = d * lax.rsqrt(jnp.mean(d * d, axis=-1, keepdims=True) + EPS)
            if vn_refs:
                vn_refs[0][rows, cols] = vn
            s = _dot(w, vn.astype(BF16)) + bias
            o_ref[rows, cols] = (_gelu_tanh(u_ref[rows, cols]) * s).astype(BF16)


def _chunk_mlp(proj, ws, b, *, chunk_mask, want_vn, tm):
    n = proj.shape[0]
    nblk = tm // CM_BLOCK
    out_shape = [jax.ShapeDtypeStruct((n, CM_W), BF16)]
    out_specs = [pl.BlockSpec((tm, CM_W), lambda i: (i, 0))]
    if want_vn:
        out_shape.append(jax.ShapeDtypeStruct((n, CM_W), F32))
        out_specs.append(pl.BlockSpec((tm, CM_W), lambda i: (i, 0)))
    return pl.pallas_call(
        functools.partial(_chunk_mlp_kernel, chunk_mask=chunk_mask, nblk=nblk),
        out_shape=out_shape,
        grid=(n // tm,),
        in_specs=[pl.BlockSpec((tm, CM_W), lambda i: (i, COL_CM_U // CM_W)),
                  pl.BlockSpec((tm, CM_W), lambda i: (i, COL_CM_V // CM_W)),
                  pl.BlockSpec((CM_GROUPS, CM_BLOCK, CM_BLOCK), lambda i: (0, 0, 0)),
                  pl.BlockSpec((CM_GROUPS, CM_BLOCK, 1), lambda i: (0, 0, 0))],
        out_specs=out_specs,
        compiler_params=_params("parallel"),
        name="chunk_mlp",
    )(proj, proj, ws, b)


def _gla_kernel(q_ref, k_ref, v_ref, r_ref, glr_ref, w2_ref, b_ref, gn_ref, s0_ref,
                o_ref, st_ref, s_scr, *, chunk, nchunk):
    c = pl.program_id(1)

    @pl.when(c == 0)
    def _():
        s_scr[...] = s0_ref[0]

    row = lax.broadcasted_iota(jnp.int32, (chunk, chunk), 0)
    col = lax.broadcasted_iota(jnp.int32, (chunk, chunk), 1)
    causal = col <= row
    tril = jnp.where(causal, 1.0, 0.0).astype(BF16)

    heads = range(GLA_HEADS)
    chunks = range(nchunk)
    x = _dot(glr_ref[...].astype(BF16), w2_ref[...]) + b_ref[...]
    hi, lo = _split_bf16(_log_sigmoid(x) * (1.0 / GLA_TAU))
    qes, kes, kds, vs, decays = {}, {}, {}, {}, {}
    for ci in chunks:
        rows = slice(ci * chunk, (ci + 1) * chunk)
        cum = _dot(tril, hi[rows, :]) + _dot(tril, lo[rows, :])
        last = cum[chunk - 1:chunk, :]
        for h in heads:
            cs = _head_cols(h)
            cum_h = cum[:, cs]
            k = k_ref[rows, cs]
            qes[ci, h] = ((q_ref[rows, cs] * (GLA_DK ** -0.5)) * jnp.exp(cum_h)).astype(BF16)
            kes[ci, h] = (k * jnp.exp(-cum_h)).astype(BF16)
            kds[ci, h] = (k * jnp.exp(last[:, cs] - cum_h)).astype(BF16)
            vs[ci, h] = v_ref[rows, cs].astype(BF16)
            decays[ci, h] = jnp.exp(last[:, cs])
    pairs = [(ci, h) for ci in chunks for h in heads]
    scores = {p: _dot_nt(qes[p], kes[p]) for p in pairs}
    updates = {p: _dot_tn(vs[p], kds[p]) for p in pairs}
    atts = {p: jnp.where(causal, scores[p], 0.0).astype(BF16) for p in pairs}
    states = [s_scr[h] for h in heads]
    for ci in chunks:
        rows = slice(ci * chunk, (ci + 1) * chunk)
        carried = [_dot_nt(qes[ci, h], states[h].astype(BF16)) for h in heads]
        local = [_dot(atts[ci, h], vs[ci, h]) for h in heads]
        for h in heads:
            cs = _head_cols(h)
            states[h] = states[h] * decays[ci, h] + updates[ci, h]
            on = _rms(local[h] + carried[h], gn_ref[:, cs])
            o_ref[rows, cs] = (on * _silu(r_ref[rows, cs])).astype(BF16)
    for h in heads:
        s_scr[h] = states[h]

    @pl.when(c == pl.num_programs(1) - 1)
    def _():
        st_ref[0] = s_scr[...]


def _gla(proj, glr, w2, b, gn, s0t, l, *, nseq, seq_len, chunk, rows_per_step):
    n = proj.shape[0]
    steps = seq_len // rows_per_step
    nchunk = rows_per_step // chunk
    w = GLA_V_W

    def rowmap(colblk):
        return lambda s, c: (s * steps + c, colblk)

    def layer(shape):
        return pl.BlockSpec((None,) + shape, lambda s, c: (l,) + (0,) * len(shape))

    return pl.pallas_call(
        functools.partial(_gla_kernel, chunk=chunk, nchunk=nchunk),
        out_shape=[jax.ShapeDtypeStruct((n, w), BF16),
                   jax.ShapeDtypeStruct((nseq, GLA_HEADS, LANES, LANES), F32)],
        grid=(nseq, steps),
        in_specs=[pl.BlockSpec((rows_per_step, w), rowmap(COL_GQ // w)),
                  pl.BlockSpec((rows_per_step, w), rowmap(COL_GK // w)),
                  pl.BlockSpec((rows_per_step, w), rowmap(COL_GV // w)),
                  pl.BlockSpec((rows_per_step, w), rowmap(COL_GR // w)),
                  pl.BlockSpec((rows_per_step, LANES), rowmap(0)),
                  layer((LANES, w)), layer((1, w)), layer((1, w)),
                  pl.BlockSpec((1, GLA_HEADS, LANES, LANES), lambda s, c: (s, 0, 0, 0))],
        out_specs=[pl.BlockSpec((rows_per_step, w), lambda s, c: (s * steps + c, 0)),
                   pl.BlockSpec((1, GLA_HEADS, LANES, LANES), lambda s, c: (s, 0, 0, 0))],
        scratch_shapes=[pltpu.VMEM((GLA_HEADS, LANES, LANES), F32)],
        compiler_params=_params("parallel", "arbitrary"),
        name="gla",
    )(proj, proj, proj, proj, glr, w2, b, gn, s0t)


def _sb_sweep(qns, blocks, carries, uo):
    heads = range(SB_HEADS)
    pairs = [(b, h) for b in range(len(blocks)) for h in heads]

    def top(x, b):
        rows = blocks[b][3]
        return x if rows is None else x[:rows]

    def put_top(x, b, new_top):
        rows = blocks[b][3]
        return new_top if rows is None else jnp.concatenate([new_top, x[rows:]], axis=0)

    zs = {(b, h): _dot_nt(top(qns[h], b), blocks[b][0][_head_rows(h, SB_BLK), :].astype(BF16)) * (SB_DIM ** -0.5)
          for b, h in pairs}
    log_betas, splits = {}, {}
    for b, h in pairs:
        mask = blocks[b][2]
        log_beta = _log_sigmoid(zs[b, h])
        log_stay = log_beta - zs[b, h]
        if mask is not None:
            log_stay = jnp.where(mask, log_stay, 0.0)
        log_betas[b, h] = log_beta
        splits[b, h] = _split_bf16(log_stay)
    sums = {p: _dot(splits[p][0], uo) + _dot(splits[p][1], uo) for p in pairs}
    carries = list(carries)
    weights = {}
    for b, h in pairs:
        mask = blocks[b][2]
        later = sums[b, h][:, :SB_BLK]
        total = sums[b, h][:, SB_BLK:]
        if carries[h] is not None:
            later = later + top(carries[h], b)
            total = put_top(carries[h], b, total + top(carries[h], b))
        carries[h] = total
        a = jnp.exp(log_betas[b, h] + later)
        if mask is not None:
            a = jnp.where(mask, a, 0.0)
        weights[b, h] = a.astype(BF16)
    outs = []
    for h in heads:
        out = None
        for b in range(len(blocks)):
            part = _dot(weights[b, h], blocks[b][1][_head_rows(h, SB_BLK), :].astype(BF16))
            out = part if out is None else put_top(out, b, top(out, b) + part)
        outs.append(out)
    return outs, carries


def _sb_queries(q_ref, gq_ref):
    return [_rms(q_ref[:, _head_cols(h)], gq_ref[...]).astype(BF16) for h in range(SB_HEADS)]


def _sb_diag_mask(tq):
    row = lax.broadcasted_iota(jnp.int32, (tq, SB_BLK), 0)
    col = lax.broadcasted_iota(jnp.int32, (tq, SB_BLK), 1)
    return col < row


def _sb_window_kernel(q_ref, gq_ref, kd_ref, vd_ref, k1_ref, v1_ref, k2_ref, v2_ref, uo_ref,
                      o_ref, mc_ref, *, tq, n_past):
    npast = pl.program_id(1) if n_past is None else n_past
    last_rows = min(tq, SB_LAST_ROWS)
    window = [(kd_ref, vd_ref, _sb_diag_mask(tq), None), (k1_ref, v1_ref, None, None),
              (k2_ref, v2_ref, None, None if last_rows == tq else last_rows)]

    def run(nblocks):
        outs, carries = _sb_sweep(_sb_queries(q_ref, gq_ref), window[:nblocks], [None] * SB_HEADS, uo_ref[...])
        done = jnp.where(npast > SB_WINDOW, 0.0, SB_DONE)
        for h in range(SB_HEADS):
            o_ref[:, _head_cols(h)] = outs[h].astype(BF16)
            if nblocks <= SB_WINDOW or last_rows == tq:
                worst = jnp.max(carries[h], axis=0, keepdims=True) + done
            else:
                worst = jnp.maximum(jnp.max(carries[h][:last_rows], axis=0, keepdims=True) + done,
                                    jnp.max(carries[h][last_rows:], axis=0, keepdims=True))
            mc_ref[0, h:h + 1, :] = worst

    if n_past is not None:
        run(1 + min(n_past, SB_WINDOW))
    else:
        for nb in range(1, SB_WINDOW + 1):
            pl.when(npast == nb - 1)(functools.partial(run, nb))
        pl.when(npast >= SB_WINDOW)(functools.partial(run, 1 + SB_WINDOW))


def _sb_window(proj, gq, kd, vd, kp, vp, uo, l, *, nseq, tq, qtiles, diag_blk0, past_blk, n_past):
    n = proj.shape[0]
    kv_blk = (SB_BLK * SB_HEADS, SB_DIM)
    past_specs = [pl.BlockSpec(kv_blk, functools.partial(lambda s, i, w: (past_blk(s, i, w), 0), w=w))
                  for w in (1, 1, 2, 2)]
    return pl.pallas_call(
        functools.partial(_sb_window_kernel, tq=tq, n_past=n_past),
        out_shape=[jax.ShapeDtypeStruct((n, SB_W), BF16),
                   jax.ShapeDtypeStruct((nseq * qtiles, SB_HEADS, LANES), F32)],
        grid=(nseq, qtiles),
        in_specs=[pl.BlockSpec((tq, SB_W), lambda s, i: (s * qtiles + i, COL_SQ // SB_W)),
                  pl.BlockSpec((None, 1, SB_DIM), lambda s, i: (l, 0, 0)),
                  pl.BlockSpec(kv_blk, lambda s, i: (diag_blk0 + s * qtiles + i, 0)),
                  pl.BlockSpec(kv_blk, lambda s, i: (diag_blk0 + s * qtiles + i, 0)),
                  *past_specs,
                  pl.BlockSpec((SB_BLK, 2 * SB_BLK), lambda s, i: (0, 0))],
        out_specs=[pl.BlockSpec((tq, SB_W), lambda s, i: (s * qtiles + i, 0)),
                   pl.BlockSpec((1, SB_HEADS, LANES), lambda s, i: (s * qtiles + i, 0, 0))],
        compiler_params=_params("parallel", "arbitrary"),
        name="sb_window",
    )(proj, gq, kd, vd, kp, vp, kp, vp, uo)


def _sb_full_kernel(flag_ref, q_ref, gq_ref, kd_ref, vd_ref, kp_ref, vp_ref, uo_ref, prev_ref,
                    o_ref, qn_scr, acc_scr, car_scr, live_scr, *, tq, qtiles, n_past):
    s, i, j = pl.program_id(0), pl.program_id(1), pl.program_id(2)
    npast = i if n_past is None else n_past
    flagged = flag_ref[s * qtiles + i] != 0

    heads = range(SB_HEADS)

    def keep(outs, carries, accumulate):
        for h in heads:
            cs = _head_cols(h)
            acc_scr[:, cs] = acc_scr[:, cs] + outs[h] if accumulate else outs[h]
            car_scr[:, cs] = carries[h]
        live_scr[0] = (jnp.max(car_scr[...]) > SB_LOG_ZERO).astype(jnp.int32)

    @pl.when(jnp.logical_and(flagged, j == 0))
    def _():
        qns = _sb_queries(q_ref, gq_ref)
        for h in heads:
            qn_scr[:, _head_cols(h)] = qns[h]
        keep(*_sb_sweep(qns, [(kd_ref, vd_ref, _sb_diag_mask(tq), None)], [None] * SB_HEADS, uo_ref[...]), False)

    @pl.when(jnp.logical_and(flagged, jnp.logical_and(j >= 1, j <= npast)))
    def _():
        @pl.when(live_scr[0] != 0)
        def _():
            qns = [qn_scr[:, _head_cols(h)] for h in heads]
            carries = [car_scr[:, _head_cols(h)] for h in heads]
            keep(*_sb_sweep(qns, [(kp_ref, vp_ref, None, None)], carries, uo_ref[...]), True)

    last = j == pl.num_programs(2) - 1

    @pl.when(jnp.logical_and(last, flagged))
    def _():
        o_ref[...] = acc_scr[...].astype(BF16)

    @pl.when(jnp.logical_and(last, jnp.logical_not(flagged)))
    def _():
        o_ref[...] = prev_ref[...]


def _sb_full(flags, proj, gq, kd, vd, kp, vp, uo, prev, l, *, nseq, tq, qtiles, diag_blk0, past_blk, n_past,
             max_past):
    n = proj.shape[0]
    kv_blk = (SB_BLK * SB_HEADS, SB_DIM)

    def past_map(s, i, j, flag_ref):
        w = jnp.clip(j, 1, max_past)
        return (jnp.where(flag_ref[s * qtiles + i] != 0, past_blk(s, i, w), past_blk(0, 0, 1)), 0)

    grid_spec = pltpu.PrefetchScalarGridSpec(
        num_scalar_prefetch=1,
        grid=(nseq, qtiles, max_past + 1),
        in_specs=[pl.BlockSpec((tq, SB_W), lambda s, i, j, f: (s * qtiles + i, COL_SQ // SB_W)),
                  pl.BlockSpec((None, 1, SB_DIM), lambda s, i, j, f: (l, 0, 0)),
                  pl.BlockSpec(kv_blk, lambda s, i, j, f: (diag_blk0 + s * qtiles + i, 0)),
                  pl.BlockSpec(kv_blk, lambda s, i, j, f: (diag_blk0 + s * qtiles + i, 0)),
                  pl.BlockSpec(kv_blk, past_map),
                  pl.BlockSpec(kv_blk, past_map),
                  pl.BlockSpec((SB_BLK, 2 * SB_BLK), lambda s, i, j, f: (0, 0)),
                  pl.BlockSpec((tq, SB_W), lambda s, i, j, f: (s * qtiles + i, 0))],
        out_specs=pl.BlockSpec((tq, SB_W), lambda s, i, j, f: (s * qtiles + i, 0)),
        scratch_shapes=[pltpu.VMEM((tq, SB_W), BF16), pltpu.VMEM((tq, SB_W), F32), pltpu.VMEM((tq, SB_W), F32),
                        pltpu.SMEM((1,), jnp.int32)])
    return pl.pallas_call(
        functools.partial(_sb_full_kernel, tq=tq, qtiles=qtiles, n_past=n_past),
        out_shape=jax.ShapeDtypeStruct((n, SB_W), BF16),
        grid_spec=grid_spec,
        compiler_params=_params("arbitrary", "arbitrary", "arbitrary"),
        name="sb_full",
    )(flags, proj, gq, kd, vd, kp, vp, uo, prev)


def _stick_breaking(proj, gq, kd, vd, kp, vp, uo, l, *, n_past, max_past, **geom):
    oc, carry = _sb_window(proj, gq, kd, vd, kp, vp, uo, l, n_past=n_past, **geom)
    flags = jnp.any(carry[:, :, 0] > SB_LOG_ZERO, axis=1).astype(jnp.int32)
    return lax.cond(jnp.any(flags != 0),
                    lambda: _sb_full(flags, proj, gq, kd, vd, kp, vp, uo, oc, l, n_past=n_past,
                                     max_past=max_past, **geom),
                    lambda: oc)


def _out_proj_kernel(x_ref, a_ref, b_ref, c_ref, w_ref, o_ref):
    o_ref[...] = (x_ref[...]
                  + _dot(a_ref[...], w_ref[0:CM_W, :])
                  + _dot(b_ref[...], w_ref[CM_W:CM_W + GLA_V_W, :])
                  + _dot(c_ref[...], w_ref[CM_W + GLA_V_W:, :]))


def _out_proj(x, oa, ob, oc, w, l, tm):
    n, d = x.shape
    return pl.pallas_call(
        _out_proj_kernel,
        out_shape=jax.ShapeDtypeStruct((n, d), F32),
        grid=(n // tm,),
        in_specs=[pl.BlockSpec((tm, d), lambda i: (i, 0)),
                  pl.BlockSpec((tm, CM_W), lambda i: (i, 0)),
                  pl.BlockSpec((tm, GLA_V_W), lambda i: (i, 0)),
                  pl.BlockSpec((tm, SB_W), lambda i: (i, 0)),
                  pl.BlockSpec((None,) + w.shape[1:], lambda i: (l, 0, 0))],
        out_specs=pl.BlockSpec((tm, d), lambda i: (i, 0)),
        compiler_params=_params("parallel"),
        name="out_proj",
    )(x, oa, ob, oc, w)


HALO = 16


def _ffn_conv(gate, prev1, prev2, cw_ref, cb_ref):
    return ((cb_ref[...] + cw_ref[0:1, :] * prev2) + cw_ref[1:2, :] * prev1) + cw_ref[2:3, :] * gate


def _ffn_weight_specs(l, d, tf, jmap):
    def spec(shape, blk):
        return pl.BlockSpec((None,) + shape, lambda *ids: (l,) + blk(jmap(*ids)))
    return [spec((1, d), lambda j: (0, 0)),
            spec((d, tf), lambda j: (0, j)),
            spec((d, tf), lambda j: (0, j)),
            spec((CONV_W, tf), lambda j: (0, j)),
            spec((1, tf), lambda j: (0, j)),
            spec((tf, d), lambda j: (j, 0))]


def _ffn_prompt_kernel(x_ref, xp_ref, g_ref, wg_ref, wu_ref, cw_ref, cb_ref, wd_ref,
                       o_ref, tail_ref, h_ref, *, tm, sub, tiles_per_seq):
    i = pl.program_id(0)
    subs = range(0, tm, sub)

    @pl.when(pl.program_id(1) == 0)
    def _():
        h_ref[0:HALO, :] = _rms(xp_ref[...], g_ref[...]).astype(BF16)
        for r0 in subs:
            x = x_ref[r0:r0 + sub, :]
            h_ref[HALO + r0:HALO + r0 + sub, :] = _rms(x, g_ref[...]).astype(BF16)
            o_ref[r0:r0 + sub, :] = x

    gates, ups = [], []
    for r0 in subs:
        h = h_ref[r0:r0 + HALO + sub, :]
        gates.append(_dot(h, wg_ref[...]))
        ups.append(_dot(h[HALO:, :], wu_ref[...]))
    halo_rows = jnp.where(i % tiles_per_seq == 0, HALO, 0)
    rowid = lax.broadcasted_iota(jnp.int32, gates[0].shape, 0)
    gates[0] = jnp.where(rowid < halo_rows, 0.0, gates[0])
    for r0, gate_all, up in zip(subs, gates, ups):
        prev1 = pltpu.roll(gate_all, 1, 0)[HALO:, :]
        prev2 = pltpu.roll(gate_all, 2, 0)[HALO:, :]
        g = _ffn_conv(gate_all[HALO:, :], prev1, prev2, cw_ref, cb_ref)
        o_ref[r0:r0 + sub, :] += _dot((_silu(g) * up).astype(BF16), wd_ref[...])
    tail_ref[0] = gates[-1][HALO + sub - 8:, :]


def _ffn_prompt(x, weights, l, *, seq_len, tm, tf, sub=512):
    n, d = x.shape
    dff = weights[1].shape[-1]
    tiles_per_seq = seq_len // tm
    halo_blocks = tm // HALO
    return pl.pallas_call(
        functools.partial(_ffn_prompt_kernel, tm=tm, sub=sub, tiles_per_seq=tiles_per_seq),
        out_shape=[jax.ShapeDtypeStruct((n, d), F32),
                   jax.ShapeDtypeStruct((n // tm, 8, dff), F32)],
        grid=(n // tm, dff // tf),
        in_specs=[pl.BlockSpec((tm, d), lambda i, j: (i, 0)),
                  pl.BlockSpec((HALO, d), lambda i, j: (jnp.maximum(i * halo_blocks - 1, 0), 0)),
                  *_ffn_weight_specs(l, d, tf, lambda i, j: j)],
        out_specs=[pl.BlockSpec((tm, d), lambda i, j: (i, 0)),
                   pl.BlockSpec((1, 8, tf), lambda i, j: (i, 0, j))],
        scratch_shapes=[pltpu.VMEM((HALO + tm, d), BF16)],
        compiler_params=_params("parallel", "arbitrary"),
        name="ffn_prompt",
    )(x, x, *weights)


def _ffn_sample_kernel(x_ref, p1_ref, p2_ref, g_ref, wg_ref, wu_ref, cw_ref, cb_ref, wd_ref,
                       o_ref, gate_ref, h_ref, *, seq_len):
    @pl.when(pl.program_id(0) == 0)
    def _():
        x = x_ref[...]
        h_ref[...] = _rms(x, g_ref[...]).astype(BF16)
        o_ref[...] = x

    h = h_ref[...]
    gate = _dot(h, wg_ref[...])
    t = jnp.bitwise_and(lax.broadcasted_iota(jnp.int32, gate.shape, 0), seq_len - 1)
    prev1 = jnp.where(t >= 1, pltpu.roll(gate, 1, 0), p1_ref[...])
    prev2 = jnp.where(t >= 2, pltpu.roll(gate, 2, 0), p2_ref[...])
    g = _ffn_conv(gate, prev1, prev2, cw_ref, cb_ref)
    up = _dot(h, wu_ref[...])
    o_ref[...] += _dot((_silu(g) * up).astype(BF16), wd_ref[...])
    gate_ref[...] = gate


def _ffn_sample(x, p1, p2, weights, l, *, seq_len, tf):
    n, d = x.shape
    dff = weights[1].shape[-1]
    assert seq_len & (seq_len - 1) == 0
    return pl.pallas_call(
        functools.partial(_ffn_sample_kernel, seq_len=seq_len),
        out_shape=[jax.ShapeDtypeStruct((n, d), F32),
                   jax.ShapeDtypeStruct((n, dff), F32)],
        grid=(dff // tf,),
        in_specs=[pl.BlockSpec((n, d), lambda j: (0, 0)),
                  pl.BlockSpec((n, tf), lambda j: (0, j)),
                  pl.BlockSpec((n, tf), lambda j: (0, j)),
                  *_ffn_weight_specs(l, d, tf, lambda j: j)],
        out_specs=[pl.BlockSpec((n, d), lambda j: (0, 0)),
                   pl.BlockSpec((n, tf), lambda j: (0, j))],
        scratch_shapes=[pltpu.VMEM((n, d), BF16)],
        compiler_params=_params("arbitrary"),
        name="ffn_sample",
    )(x, p1, p2, *weights)


def _pack_w_in(w_in):
    nl, d, _ = w_in.shape
    o = 0
    seg = {}
    for name, width in (("cu", CM_W), ("cv", CM_W), ("gq", GLA_HEADS * GLA_DK), ("gk", GLA_HEADS * GLA_DK),
                        ("gv", GLA_V_W), ("gr", GLA_V_W), ("glr", GLA_RANK), ("sq", SB_W), ("sk", SB_W), ("sv", SB_W)):
        seg[name] = w_in[:, :, o:o + width]
        o += width

    def pad_heads(w):
        w = w.reshape(nl, d, GLA_HEADS, GLA_DK)
        return jnp.pad(w, ((0, 0), (0, 0), (0, 0), (0, LANES - GLA_DK))).reshape(nl, d, GLA_HEADS * LANES)

    w_all = jnp.concatenate([seg["cu"], seg["cv"], pad_heads(seg["gq"]), pad_heads(seg["gk"]),
                             seg["gv"], seg["gr"], seg["sq"], seg["sk"], seg["sv"]], axis=-1)
    w_glr = jnp.pad(seg["glr"], ((0, 0), (0, 0), (0, LANES - GLA_RANK)))
    return w_all.astype(BF16), w_glr.astype(BF16)


def _pack_gla_gate(gla_w2, gla_b):
    nl = gla_w2.shape[0]
    w2 = gla_w2.reshape(nl, GLA_RANK, GLA_HEADS, GLA_DK)
    w2 = jnp.pad(w2, ((0, 0), (0, LANES - GLA_RANK), (0, 0), (0, LANES - GLA_DK)))
    b = jnp.pad(gla_b.reshape(nl, 1, GLA_HEADS, GLA_DK), ((0, 0), (0, 0), (0, 0), (0, LANES - GLA_DK)))
    return w2.reshape(nl, LANES, GLA_HEADS * LANES).astype(BF16), b.reshape(nl, 1, GLA_HEADS * LANES)


def _state_to_kernel(s):
    st = jnp.swapaxes(s, 2, 3)
    return jnp.pad(st, ((0, 0), (0, 0), (0, 0), (0, LANES - GLA_DK)))


def _state_from_kernel(st):
    return jnp.swapaxes(st, 2, 3)[:, :, :GLA_DK, :]


def _later_sum_matrix():
    j = lax.broadcasted_iota(jnp.int32, (SB_BLK, 2 * SB_BLK), 0)
    s = lax.broadcasted_iota(jnp.int32, (SB_BLK, 2 * SB_BLK), 1)
    return jnp.where(jnp.logical_or(j > s, s >= SB_BLK), 1.0, 0.0).astype(BF16)


def _layer_prompt(x, lw, l, kv_bufs, *, nseq, seq_len, depth):
    n = x.shape[0]
    proj, k_all, v_all, glr = _in_proj(x, lw["norm1_g"], lw["w_all"], lw["w_glr"], lw["sb_k_g"], l, TM,
                                       kv_rows=depth * n, kv_row0=l * n, kv_bufs=kv_bufs)
    (oa,) = _chunk_mlp(proj, lw["cm_ws"][l], lw["cm_b"][l], chunk_mask=True, want_vn=False, tm=TM)
    s0t = jnp.zeros((nseq, GLA_HEADS, LANES, LANES), F32)
    ob, st = _gla(proj, glr, lw["gla_w2"], lw["gla_b"], lw["gla_gn"], s0t, l,
                  nseq=nseq, seq_len=seq_len, chunk=CHUNK, rows_per_step=TM)
    qtiles = seq_len // SB_BLK
    blk0 = l * n // SB_BLK

    def past_blk(s, i, w):
        return blk0 + s * qtiles + jnp.maximum(i - w, 0)

    oc = _stick_breaking(proj, lw["sb_q_g"], k_all, v_all, k_all, v_all, lw["uo"], l, nseq=nseq, tq=SB_BLK,
                         qtiles=qtiles, diag_blk0=blk0, past_blk=past_blk, n_past=None, max_past=qtiles - 1)
    x = _out_proj(x, oa, ob, oc, lw["w_out"], l, TM)
    x, tail = _ffn_prompt(x, lw["ffn"], l, seq_len=seq_len, tm=TM_FFN, tf=TF, sub=TM)
    tiles_per_seq = seq_len // TM_FFN
    conv_state = tail[tiles_per_seq - 1::tiles_per_seq, 8 - (CONV_W - 1):, :]
    return x, (k_all, v_all), _state_from_kernel(st), conv_state


def _layer_sample(x, lw, l, cache_k, cache_v, gla_s0, conv_prev, *, nseq, seq_len, past_len):
    n = x.shape[0]
    proj, k, v, glr = _in_proj(x, lw["norm1_g"], lw["w_all"], lw["w_glr"], lw["sb_k_g"], l, n,
                               kv_rows=n, kv_row0=0, kv_bufs=None)
    ws = lw["cm_ws"][l][:, :seq_len, :seq_len]
    ws_bd = jnp.einsum("ab,gij->gaibj", jnp.eye(nseq, dtype=F32), ws).reshape(CM_GROUPS, n, n)
    b_bd = jnp.tile(lw["cm_b"][l][:, :seq_len, :], (1, nseq, 1))
    oa, cm_v = _chunk_mlp(proj, ws_bd, b_bd, chunk_mask=False, want_vn=True, tm=n)
    ob, st = _gla(proj, glr, lw["gla_w2"], lw["gla_b"], lw["gla_gn"], _state_to_kernel(gla_s0), l,
                  nseq=nseq, seq_len=seq_len, chunk=seq_len, rows_per_step=seq_len)

    def pad_new(a):
        a = jnp.pad(a.reshape(nseq, seq_len, SB_HEADS, SB_DIM), ((0, 0), (0, SB_BLK - seq_len), (0, 0), (0, 0)))
        return a.reshape(nseq * SB_BLK * SB_HEADS, SB_DIM)

    past_blocks = past_len // SB_BLK

    def past_blk(s, i, w):
        return (l * nseq + s) * past_blocks + past_blocks - w

    oc = _stick_breaking(proj, lw["sb_q_g"], pad_new(k), pad_new(v), cache_k, cache_v, lw["uo"], l, nseq=nseq,
                         tq=seq_len, qtiles=1, diag_blk0=0, past_blk=past_blk, n_past=past_blocks,
                         max_past=past_blocks)
    x = _out_proj(x, oa, ob, oc, lw["w_out"], l, n)
    dff = conv_prev.shape[-1]
    p = jnp.zeros((nseq, seq_len, dff), F32)
    p1 = p.at[:, 0].set(conv_prev[:, 1]).reshape(n, dff)
    p2 = p.at[:, 0].set(conv_prev[:, 0]).at[:, 1].set(conv_prev[:, 1]).reshape(n, dff)
    x, gate = _ffn_sample(x, p1, p2, lw["ffn"], l, seq_len=seq_len, tf=TF)
    conv_state = gate.reshape(nseq, seq_len, dff)[:, seq_len - (CONV_W - 1):, :]
    return x, k, v, _state_from_kernel(st), conv_state, cm_v


def kernel(x_prompt, x_sample, cache_sb_k, cache_sb_v, state_gla, state_ffn_conv, norm1_g, w_in, cm_ws, cm_b,
           gla_w2, gla_b, gla_norm_g, sb_q_g, sb_k_g, w_out, norm2_g, ffn_w_gate, ffn_w_up, ffn_conv_w,
           ffn_conv_b, ffn_w_down):
    depth = w_in.shape[0]
    bp, tp, d = x_prompt.shape
    bs, ts, _ = x_sample.shape
    past_len = cache_sb_k.shape[2]
    assert d == D_MODEL and bs * ts == CM_BLOCK and tp % TM_FFN == 0 and past_len % SB_BLK == 0

    w_all, w_glr = _pack_w_in(w_in)
    w2, b2 = _pack_gla_gate(gla_w2, gla_b)
    lw = {
        "norm1_g": norm1_g[:, None, :], "w_all": w_all, "w_glr": w_glr,
        "cm_ws": cm_ws, "cm_b": cm_b[..., None],
        "gla_w2": w2, "gla_b": b2, "gla_gn": gla_norm_g.reshape(depth, 1, GLA_V_W),
        "sb_q_g": sb_q_g[:, None, :], "sb_k_g": sb_k_g[:, None, :], "uo": _later_sum_matrix(),
        "w_out": w_out.astype(BF16),
        "ffn": (norm2_g[:, None, :], ffn_w_gate.astype(BF16), ffn_w_up.astype(BF16), ffn_conv_w,
                ffn_conv_b[:, None, :], ffn_w_down.astype(BF16)),
    }
    cache_k = cache_sb_k.reshape(depth * bs * past_len * SB_HEADS, SB_DIM)
    cache_v = cache_sb_v.reshape(depth * bs * past_len * SB_HEADS, SB_DIM)

    yp = x_prompt.reshape(bp * tp, d)
    ys = x_sample.reshape(bs * ts, d)
    kv_p = None
    outs_p, outs_s = [], []
    for l in range(depth):
        yp, kv_p, sp, cp = _layer_prompt(yp, lw, l, kv_p, nseq=bp, seq_len=tp, depth=depth)
        ys, ks, vs, ss, cs, cmv = _layer_sample(ys, lw, l, cache_k, cache_v, state_gla[l], state_ffn_conv[l],
                                                nseq=bs, seq_len=ts, past_len=past_len)
        outs_p.append((sp, cp))
        outs_s.append((ks.reshape(bs, ts, SB_HEADS, SB_DIM), vs.reshape(bs, ts, SB_HEADS, SB_DIM), ss, cs,
                       cmv.reshape(bs, ts, CM_GROUPS, CM_DIM)))

    stack = lambda outs, i: jnp.stack([o[i] for o in outs])
    kv_shape = (depth, bp, tp, SB_HEADS, SB_DIM)
    return (yp.reshape(bp, tp, d), ys.reshape(bs, ts, d),
            kv_p[0].reshape(kv_shape), kv_p[1].reshape(kv_shape), stack(outs_p, 0), stack(outs_p, 1),
            stack(outs_s, 0), stack(outs_s, 1), stack(outs_s, 2), stack(outs_s, 3), stack(outs_s, 4))
```

```python
import functools

import jax
import jax.numpy as jnp
from jax import lax
from jax.experimental import pallas as pl
from jax.experimental.pallas import tpu as pltpu

F32 = jnp.float32
BF16 = jnp.bfloat16

D_MODEL = 2048
CHUNK = 64
CM_BLOCK = 128
CM_GROUPS = 4
CM_DIM = 128
CM_W = CM_GROUPS * CM_DIM
GLA_HEADS = 4
GLA_DK = 64
GLA_DV = 128
GLA_RANK = 16
GLA_TAU = 16.0
GLA_V_W = GLA_HEADS * GLA_DV
SB_HEADS = 8
SB_DIM = 128
SB_W = SB_HEADS * SB_DIM
D_FF = 5632
CONV_W = 3
EPS = 1e-6

LANES = 128
V7X_VMEM_LIMIT = 56 * 2 ** 20

TM = 512
TM_FFN = 1024
TN = 2048
TF = 512

COL_CM_U = 0
COL_CM_V = 512
COL_GQ = 1024
COL_GK = 1536
COL_GV = 2048
COL_GR = 2560
COL_SQ = 3072
PROJ_W = 4096
PROJ_TILES = PROJ_W // TN
KV_TILES = 1
assert 2 * SB_W == TN

SB_BLK = 128
SB_WINDOW = 2
SB_LAST_ROWS = SB_BLK
SB_LOG_ZERO = -105.0
SB_DONE = -1e30


def _dot(a, b):
    return jnp.dot(a, b, preferred_element_type=F32)


def _dot_nt(a, b):
    return lax.dot_general(a, b, (((1,), (1,)), ((), ())), preferred_element_type=F32)


def _dot_tn(a, b):
    return lax.dot_general(a, b, (((0,), (0,)), ((), ())), preferred_element_type=F32)


def _rms(x, g):
    ms = jnp.mean(x * x, axis=-1, keepdims=True)
    return x * lax.rsqrt(ms + EPS) * g


def _gelu_tanh(x):
    return x * (0.5 * (1.0 + jnp.tanh(0.7978845608028654 * (x + 0.044715 * (x * x * x)))))


def _log_sigmoid(x):
    return jnp.minimum(x, 0.0) - jnp.log(1.0 + jnp.exp(-jnp.abs(x)))


def _silu(x):
    return x * (1.0 / (1.0 + jnp.exp(-x)))


def _split_bf16(x):
    hi = x.astype(BF16)
    lo = (x - hi.astype(F32)).astype(BF16)
    return hi, lo


def _params(*sem):
    return pltpu.CompilerParams(dimension_semantics=sem, vmem_limit_bytes=V7X_VMEM_LIMIT)


def _head_cols(h):
    return slice(h * LANES, (h + 1) * LANES)


def _head_rows(h, ntok):
    return pl.ds(h, ntok, stride=SB_HEADS)


def _in_proj_kernel(*refs, n_alias):
    x_ref, g_ref, w_ref, wl_ref, gk_ref = refs[:5]
    proj_ref, k_ref, v_ref, glr_ref, h_ref = refs[5 + n_alias:]
    j = pl.program_id(1)

    @pl.when(j == 0)
    def _():
        h = _rms(x_ref[...], g_ref[...]).astype(BF16)
        h_ref[...] = h
        glr_ref[...] = _dot(h, wl_ref[...])

    @pl.when(j < PROJ_TILES)
    def _():
        proj_ref[...] = _dot(h_ref[...], w_ref[...])

    @pl.when(j == PROJ_TILES)
    def _():
        kv = _dot(h_ref[...], w_ref[...])
        rows = kv.shape[0]
        for h in range(SB_HEADS):
            k_ref[_head_rows(h, rows), :] = _rms(kv[:, _head_cols(h)], gk_ref[...])
        for h in range(SB_HEADS):
            v_ref[_head_rows(h, rows), :] = kv[:, _head_cols(SB_HEADS + h)]


def _in_proj(x, g, w_all, w_glr, gk, l, tm, *, kv_rows, kv_row0, kv_bufs):
    n, d = x.shape
    blk0 = kv_row0 // tm
    in_specs = [pl.BlockSpec((tm, d), lambda i, j: (i, 0)),
                pl.BlockSpec((None, 1, d), lambda i, j: (l, 0, 0)),
                pl.BlockSpec((None, d, TN), lambda i, j: (l, 0, j)),
                pl.BlockSpec((None, d, LANES), lambda i, j: (l, 0, 0)),
                pl.BlockSpec((None, 1, SB_DIM), lambda i, j: (l, 0, 0))]
    args = [x, g, w_all, w_glr, gk]
    aliases = {}
    if kv_bufs is not None:
        in_specs += [pl.BlockSpec(memory_space=pl.ANY)] * 2
        args += list(kv_bufs)
        aliases = {5: 1, 6: 2}
    kv_shape = jax.ShapeDtypeStruct((kv_rows * SB_HEADS, SB_DIM), F32)
    kv_spec = pl.BlockSpec((tm * SB_HEADS, SB_DIM), lambda i, j: (blk0 + i, 0))
    return pl.pallas_call(
        functools.partial(_in_proj_kernel, n_alias=len(aliases)),
        out_shape=[jax.ShapeDtypeStruct((n, PROJ_W), F32), kv_shape, kv_shape,
                   jax.ShapeDtypeStruct((n, LANES), F32)],
        grid=(n // tm, PROJ_TILES + KV_TILES),
        in_specs=in_specs,
        out_specs=[pl.BlockSpec((tm, TN), lambda i, j: (i, jnp.minimum(j, PROJ_TILES - 1))),
                   kv_spec, kv_spec,
                   pl.BlockSpec((tm, LANES), lambda i, j: (i, 0))],
        scratch_shapes=[pltpu.VMEM((tm, d), BF16)],
        input_output_aliases=aliases,
        compiler_params=_params("parallel", "arbitrary"),
        name="in_proj",
    )(*args)


def _chunk_mlp_kernel(u_ref, v_ref, ws_ref, b_ref, o_ref, *vn_refs, chunk_mask, nblk):
    row = lax.broadcasted_iota(jnp.int32, (CM_BLOCK, CM_BLOCK), 0)
    col = lax.broadcasted_iota(jnp.int32, (CM_BLOCK, CM_BLOCK), 1)
    visible = jnp.logical_or(row >= CHUNK, col < CHUNK)
    for g in range(CM_GROUPS):
        w = ws_ref[g]
        if chunk_mask:
            w = jnp.where(visible, w, 0.0)
        w = w.astype(BF16)
        bias = b_ref[g]
        cols = slice(g * CM_DIM, (g + 1) * CM_DIM)
        for n in range(nblk):
            rows = slice(n * CM_BLOCK, (n + 1) * CM_BLOCK)
            v = _gelu_tanh(v_ref[rows, cols])
            d = v - jnp.mean(v, axis=-1, keepdims=True)
            vn = d * lax.rsqrt(jnp.mean(d * d, axis=-1, keepdims=True) + EPS)
            if vn_refs:
                vn_refs[0][rows, cols] = vn
            s = _dot(w, vn.astype(BF16)) + bias
            o_ref[rows, cols] = (_gelu_tanh(u_ref[rows, cols]) * s).astype(BF16)


def _chunk_mlp(proj, ws, b, *, chunk_mask, want_vn, tm):
    n = proj.shape[0]
    nblk = tm // CM_BLOCK
    out_shape = [jax.ShapeDtypeStruct((n, CM_W), BF16)]
    out_specs = [pl.BlockSpec((tm, CM_W), lambda i: (i, 0))]
    if want_vn:
        out_shape.append(jax.ShapeDtypeStruct((n, CM_W), F32))
        out_specs.append(pl.BlockSpec((tm, CM_W), lambda i: (i, 0)))
    return pl.pallas_call(
        functools.partial(_chunk_mlp_kernel, chunk_mask=chunk_mask, nblk=nblk),
        out_shape=out_shape,
        grid=(n // tm,),
        in_specs=[pl.BlockSpec((tm, CM_W), lambda i: (i, COL_CM_U // CM_W)),
                  pl.BlockSpec((tm, CM_W), lambda i: (i, COL_CM_V // CM_W)),
                  pl.BlockSpec((CM_GROUPS, CM_BLOCK, CM_BLOCK), lambda i: (0, 0, 0)),
                  pl.BlockSpec((CM_GROUPS, CM_BLOCK, 1), lambda i: (0, 0, 0))],
        out_specs=out_specs,
        compiler_params=_params("parallel"),
        name="chunk_mlp",
    )(proj, proj, ws, b)


def _gla_kernel(q_ref, k_ref, v_ref, r_ref, glr_ref, w2_ref, b_ref, gn_ref, s0_ref,
                o_ref, st_ref, s_scr, *, chunk, nchunk):
    c = pl.program_id(1)

    @pl.when(c == 0)
    def _():
        s_scr[...] = s0_ref[0]

    row = lax.broadcasted_iota(jnp.int32, (chunk, chunk), 0)
    col = lax.broadcasted_iota(jnp.int32, (chunk, chunk), 1)
    causal = col <= row
    tril = jnp.where(causal, 1.0, 0.0).astype(BF16)

    heads = range(GLA_HEADS)
    chunks = range(nchunk)
    x = _dot(glr_ref[...].astype(BF16), w2_ref[...]) + b_ref[...]
    hi, lo = _split_bf16(_log_sigmoid(x) * (1.0 / GLA_TAU))
    qes, kes, kds, vs, decays = {}, {}, {}, {}, {}
    for ci in chunks:
        rows = slice(ci * chunk, (ci + 1) * chunk)
        cum = _dot(tril, hi[rows, :]) + _dot(tril, lo[rows, :])
        last = cum[chunk - 1:chunk, :]
        for h in heads:
            cs = _head_cols(h)
            cum_h = cum[:, cs]
            k = k_ref[rows, cs]
            qes[ci, h] = ((q_ref[rows, cs] * (GLA_DK ** -0.5)) * jnp.exp(cum_h)).astype(BF16)
            kes[ci, h] = (k * jnp.exp(-cum_h)).astype(BF16)
            kds[ci, h] = (k * jnp.exp(last[:, cs] - cum_h)).astype(BF16)
            vs[ci, h] = v_ref[rows, cs].astype(BF16)
            decays[ci, h] = jnp.exp(last[:, cs])
    pairs = [(ci, h) for ci in chunks for h in heads]
    scores = {p: _dot_nt(qes[p], kes[p]) for p in pairs}
    updates = {p: _dot_tn(vs[p], kds[p]) for p in pairs}
    atts = {p: jnp.where(causal, scores[p], 0.0).astype(BF16) for p in pairs}
    states = [s_scr[h] for h in heads]
    for ci in chunks:
        rows = slice(ci * chunk, (ci + 1) * chunk)
        carried = [_dot_nt(qes[ci, h], states[h].astype(BF16)) for h in heads]
        local = [_dot(atts[ci, h], vs[ci, h]) for h in heads]
        for h in heads:
            cs = _head_cols(h)
            states[h] = states[h] * decays[ci, h] + updates[ci, h]
            on = _rms(local[h] + carried[h], gn_ref[:, cs])
            o_ref[rows, cs] = (on * _silu(r_ref[rows, cs])).astype(BF16)
    for h in heads:
        s_scr[h] = states[h]

    @pl.when(c == pl.num_programs(1) - 1)
    def _():
        st_ref[0] = s_scr[...]


def _gla(proj, glr, w2, b, gn, s0t, l, *, nseq, seq_len, chunk, rows_per_step):
    n = proj.shape[0]
    steps = seq_len // rows_per_step
    nchunk = rows_per_step // chunk
    w = GLA_V_W

    def rowmap(colblk):
        return lambda s, c: (s * steps + c, colblk)

    def layer(shape):
        return pl.BlockSpec((None,) + shape, lambda s, c: (l,) + (0,) * len(shape))

    return pl.pallas_call(
        functools.partial(_gla_kernel, chunk=chunk, nchunk=nchunk),
        out_shape=[jax.ShapeDtypeStruct((n, w), BF16),
                   jax.ShapeDtypeStruct((nseq, GLA_HEADS, LANES, LANES), F32)],
        grid=(nseq, steps),
        in_specs=[pl.BlockSpec((rows_per_step, w), rowmap(COL_GQ // w)),
                  pl.BlockSpec((rows_per_step, w), rowmap(COL_GK // w)),
                  pl.BlockSpec((rows_per_step, w), rowmap(COL_GV // w)),
                  pl.BlockSpec((rows_per_step, w), rowmap(COL_GR // w)),
                  pl.BlockSpec((rows_per_step, LANES), rowmap(0)),
                  layer((LANES, w)), layer((1, w)), layer((1, w)),
                  pl.BlockSpec((1, GLA_HEADS, LANES, LANES), lambda s, c: (s, 0, 0, 0))],
        out_specs=[pl.BlockSpec((rows_per_step, w), lambda s, c: (s * steps + c, 0)),
                   pl.BlockSpec((1, GLA_HEADS, LANES, LANES), lambda s, c: (s, 0, 0, 0))],
        scratch_shapes=[pltpu.VMEM((GLA_HEADS, LANES, LANES), F32)],
        compiler_params=_params("parallel", "arbitrary"),
        name="gla",
    )(proj, proj, proj, proj, glr, w2, b, gn, s0t)


def _sb_sweep(qns, blocks, carries, uo):
    heads = range(SB_HEADS)
    pairs = [(b, h) for b in range(len(blocks)) for h in heads]

    def top(x, b):
        rows = blocks[b][3]
        return x if rows is None else x[:rows]

    def put_top(x, b, new_top):
        rows = blocks[b][3]
        return new_top if rows is None else jnp.concatenate([new_top, x[rows:]], axis=0)

    zs = {(b, h): _dot_nt(top(qns[h], b), blocks[b][0][_head_rows(h, SB_BLK), :].astype(BF16)) * (SB_DIM ** -0.5)
          for b, h in pairs}
    log_betas, splits = {}, {}
    for b, h in pairs:
        mask = blocks[b][2]
        log_beta = _log_sigmoid(zs[b, h])
        log_stay = log_beta - zs[b, h]
        if mask is not None:
            log_stay = jnp.where(mask, log_stay, 0.0)
        log_betas[b, h] = log_beta
        splits[b, h] = _split_bf16(log_stay)
    sums = {p: _dot(splits[p][0], uo) + _dot(splits[p][1], uo) for p in pairs}
    carries = list(carries)
    weights = {}
    for b, h in pairs:
        mask = blocks[b][2]
        later = sums[b, h][:, :SB_BLK]
        total = sums[b, h][:, SB_BLK:]
        if carries[h] is not None:
            later = later + top(carries[h], b)
            total = put_top(carries[h], b, total + top(carries[h], b))
        carries[h] = total
        a = jnp.exp(log_betas[b, h] + later)
        if mask is not None:
            a = jnp.where(mask, a, 0.0)
        weights[b, h] = a.astype(BF16)
    outs = []
    for h in heads:
        out = None
        for b in range(len(blocks)):
            part = _dot(weights[b, h], blocks[b][1][_head_rows(h, SB_BLK), :].astype(BF16))
            out = part if out is None else put_top(out, b, top(out, b) + part)
        outs.append(out)
    return outs, carries


def _sb_queries(q_ref, gq_ref):
    return [_rms(q_ref[:, _head_cols(h)], gq_ref[...]).astype(BF16) for h in range(SB_HEADS)]


def _sb_diag_mask(tq):
    row = lax.broadcasted_iota(jnp.int32, (tq, SB_BLK), 0)
    col = lax.broadcasted_iota(jnp.int32, (tq, SB_BLK), 1)
    return col < row


def _sb_window_kernel(q_ref, gq_ref, kd_ref, vd_ref, k1_ref, v1_ref, k2_ref, v2_ref, uo_ref,
                      o_ref, mc_ref, *, tq, n_past):
    npast = pl.program_id(1) if n_past is None else n_past
    last_rows = min(tq, SB_LAST_ROWS)
    window = [(kd_ref, vd_ref, _sb_diag_mask(tq), None), (k1_ref, v1_ref, None, None),
              (k2_ref, v2_ref, None, None if last_rows == tq else last_rows)]

    def run(nblocks):
        outs, carries = _sb_sweep(_sb_queries(q_ref, gq_ref), window[:nblocks], [None] * SB_HEADS, uo_ref[...])
        done = jnp.where(npast > SB_WINDOW, 0.0, SB_DONE)
        for h in range(SB_HEADS):
            o_ref[:, _head_cols(h)] = outs[h].astype(BF16)
            if nblocks <= SB_WINDOW or last_rows == tq:
                worst = jnp.max(carries[h], axis=0, keepdims=True) + done
            else:
                worst = jnp.maximum(jnp.max(carries[h][:last_rows], axis=0, keepdims=True) + done,
                                    jnp.max(carries[h][last_rows:], axis=0, keepdims=True))
            mc_ref[0, h:h + 1, :] = worst

    if n_past is not None:
        run(1 + min(n_past, SB_WINDOW))
    else:
        for nb in range(1, SB_WINDOW + 1):
            pl.when(npast == nb - 1)(functools.partial(run, nb))
        pl.when(npast >= SB_WINDOW)(functools.partial(run, 1 + SB_WINDOW))


def _sb_window(proj, gq, kd, vd, kp, vp, uo, l, *, nseq, tq, qtiles, diag_blk0, past_blk, n_past):
    n = proj.shape[0]
    kv_blk = (SB_BLK * SB_HEADS, SB_DIM)
    past_specs = [pl.BlockSpec(kv_blk, functools.partial(lambda s, i, w: (past_blk(s, i, w), 0), w=w))
                  for w in (1, 1, 2, 2)]
    return pl.pallas_call(
        functools.partial(_sb_window_kernel, tq=tq, n_past=n_past),
        out_shape=[jax.ShapeDtypeStruct((n, SB_W), BF16),
                   jax.ShapeDtypeStruct((nseq * qtiles, SB_HEADS, LANES), F32)],
        grid=(nseq, qtiles),
        in_specs=[pl.BlockSpec((tq, SB_W), lambda s, i: (s * qtiles + i, COL_SQ // SB_W)),
                  pl.BlockSpec((None, 1, SB_DIM), lambda s, i: (l, 0, 0)),
                  pl.BlockSpec(kv_blk, lambda s, i: (diag_blk0 + s * qtiles + i, 0)),
                  pl.BlockSpec(kv_blk, lambda s, i: (diag_blk0 + s * qtiles + i, 0)),
                  *past_specs,
                  pl.BlockSpec((SB_BLK, 2 * SB_BLK), lambda s, i: (0, 0))],
        out_specs=[pl.BlockSpec((tq, SB_W), lambda s, i: (s * qtiles + i, 0)),
                   pl.BlockSpec((1, SB_HEADS, LANES), lambda s, i: (s * qtiles + i, 0, 0))],
        compiler_params=_params("parallel", "arbitrary"),
        name="sb_window",
    )(proj, gq, kd, vd, kp, vp, kp, vp, uo)


def _sb_full_kernel(flag_ref, q_ref, gq_ref, kd_ref, vd_ref, kp_ref, vp_ref, uo_ref, prev_ref,
                    o_ref, qn_scr, acc_scr, car_scr, live_scr, *, tq, qtiles, n_past):
    s, i, j = pl.program_id(0), pl.program_id(1), pl.program_id(2)
    npast = i if n_past is None else n_past
    flagged = flag_ref[s * qtiles + i] != 0

    heads = range(SB_HEADS)

    def keep(outs, carries, accumulate):
        for h in heads:
            cs = _head_cols(h)
            acc_scr[:, cs] = acc_scr[:, cs] + outs[h] if accumulate else outs[h]
            car_scr[:, cs] = carries[h]
        live_scr[0] = (jnp.max(car_scr[...]) > SB_LOG_ZERO).astype(jnp.int32)

    @pl.when(jnp.logical_and(flagged, j == 0))
    def _():
        qns = _sb_queries(q_ref, gq_ref)
        for h in heads:
            qn_scr[:, _head_cols(h)] = qns[h]
        keep(*_sb_sweep(qns, [(kd_ref, vd_ref, _sb_diag_mask(tq), None)], [None] * SB_HEADS, uo_ref[...]), False)

    @pl.when(jnp.logical_and(flagged, jnp.logical_and(j >= 1, j <= npast)))
    def _():
        @pl.when(live_scr[0] != 0)
        def _():
            qns = [qn_scr[:, _head_cols(h)] for h in heads]
            carries = [car_scr[:, _head_cols(h)] for h in heads]
            keep(*_sb_sweep(qns, [(kp_ref, vp_ref, None, None)], carries, uo_ref[...]), True)

    last = j == pl.num_programs(2) - 1

    @pl.when(jnp.logical_and(last, flagged))
    def _():
        o_ref[...] = acc_scr[...].astype(BF16)

    @pl.when(jnp.logical_and(last, jnp.logical_not(flagged)))
    def _():
        o_ref[...] = prev_ref[...]


def _sb_full(flags, proj, gq, kd, vd, kp, vp, uo, prev, l, *, nseq, tq, qtiles, diag_blk0, past_blk, n_past,
             max_past):
    n = proj.shape[0]
    kv_blk = (SB_BLK * SB_HEADS, SB_DIM)

    def past_map(s, i, j, flag_ref):
        w = jnp.clip(j, 1, max_past)
        return (jnp.where(flag_ref[s * qtiles + i] != 0, past_blk(s, i, w), past_blk(0, 0, 1)), 0)

    grid_spec = pltpu.PrefetchScalarGridSpec(
        num_scalar_prefetch=1,
        grid=(nseq, qtiles, max_past + 1),
        in_specs=[pl.BlockSpec((tq, SB_W), lambda s, i, j, f: (s * qtiles + i, COL_SQ // SB_W)),
                  pl.BlockSpec((None, 1, SB_DIM), lambda s, i, j, f: (l, 0, 0)),
                  pl.BlockSpec(kv_blk, lambda s, i, j, f: (diag_blk0 + s * qtiles + i, 0)),
                  pl.BlockSpec(kv_blk, lambda s, i, j, f: (diag_blk0 + s * qtiles + i, 0)),
                  pl.BlockSpec(kv_blk, past_map),
                  pl.BlockSpec(kv_blk, past_map),
                  pl.BlockSpec((SB_BLK, 2 * SB_BLK), lambda s, i, j, f: (0, 0)),
                  pl.BlockSpec((tq, SB_W), lambda s, i, j, f: (s * qtiles + i, 0))],
        out_specs=pl.BlockSpec((tq, SB_W), lambda s, i, j, f: (s * qtiles + i, 0)),
        scratch_shapes=[pltpu.VMEM((tq, SB_W), BF16), pltpu.VMEM((tq, SB_W), F32), pltpu.VMEM((tq, SB_W), F32),
                        pltpu.SMEM((1,), jnp.int32)])
    return pl.pallas_call(
        functools.partial(_sb_full_kernel, tq=tq, qtiles=qtiles, n_past=n_past),
        out_shape=jax.ShapeDtypeStruct((n, SB_W), BF16),
        grid_spec=grid_spec,
        compiler_params=_params("arbitrary", "arbitrary", "arbitrary"),
        name="sb_full",
    )(flags, proj, gq, kd, vd, kp, vp, uo, prev)


def _stick_breaking(proj, gq, kd, vd, kp, vp, uo, l, *, n_past, max_past, **geom):
    oc, carry = _sb_window(proj, gq, kd, vd, kp, vp, uo, l, n_past=n_past, **geom)
    flags = jnp.any(carry[:, :, 0] > SB_LOG_ZERO, axis=1).astype(jnp.int32)
    return lax.cond(jnp.any(flags != 0),
                    lambda: _sb_full(flags, proj, gq, kd, vd, kp, vp, uo, oc, l, n_past=n_past,
                                     max_past=max_past, **geom),
                    lambda: oc)


def _out_proj_kernel(x_ref, a_ref, b_ref, c_ref, w_ref, o_ref):
    o_ref[...] = (x_ref[...]
                  + _dot(a_ref[...], w_ref[0:CM_W, :])
                  + _dot(b_ref[...], w_ref[CM_W:CM_W + GLA_V_W, :])
                  + _dot(c_ref[...], w_ref[CM_W + GLA_V_W:, :]))


def _out_proj(x, oa, ob, oc, w, l, tm):
    n, d = x.shape
    return pl.pallas_call(
        _out_proj_kernel,
        out_shape=jax.ShapeDtypeStruct((n, d), F32),
        grid=(n // tm,),
        in_specs=[pl.BlockSpec((tm, d), lambda i: (i, 0)),
                  pl.BlockSpec((tm, CM_W), lambda i: (i, 0)),
                  pl.BlockSpec((tm, GLA_V_W), lambda i: (i, 0)),
                  pl.BlockSpec((tm, SB_W), lambda i: (i, 0)),
                  pl.BlockSpec((None,) + w.shape[1:], lambda i: (l, 0, 0))],
        out_specs=pl.BlockSpec((tm, d), lambda i: (i, 0)),
        compiler_params=_params("parallel"),
        name="out_proj",
    )(x, oa, ob, oc, w)


HALO = 16


def _ffn_conv(gate, prev1, prev2, cw_ref, cb_ref):
    return ((cb_ref[...] + cw_ref[0:1, :] * prev2) + cw_ref[1:2, :] * prev1) + cw_ref[2:3, :] * gate


def _ffn_weight_specs(l, d, tf, jmap):
    def spec(shape, blk):
        return pl.BlockSpec((None,) + shape, lambda *ids: (l,) + blk(jmap(*ids)))
    return [spec((1, d), lambda j: (0, 0)),
            spec((d, tf), lambda j: (0, j)),
            spec((d, tf), lambda j: (0, j)),
            spec((CONV_W, tf), lambda j: (0, j)),
            spec((1, tf), lambda j: (0, j)),
            spec((tf, d), lambda j: (j, 0))]


def _ffn_prompt_kernel(x_ref, xp_ref, g_ref, wg_ref, wu_ref, cw_ref, cb_ref, wd_ref,
                       o_ref, tail_ref, h_ref, *, tm, sub, tiles_per_seq):
    i = pl.program_id(0)
    subs = range(0, tm, sub)

    @pl.when(pl.program_id(1) == 0)
    def _():
        h_ref[0:HALO, :] = _rms(xp_ref[...], g_ref[...]).astype(BF16)
        for r0 in subs:
            x = x_ref[r0:r0 + sub, :]
            h_ref[HALO + r0:HALO + r0 + sub, :] = _rms(x, g_ref[...]).astype(BF16)
            o_ref[r0:r0 + sub, :] = x

    gates, ups = [], []
    for r0 in subs:
        h = h_ref[r0:r0 + HALO + sub, :]
        gates.append(_dot(h, wg_ref[...]))
        ups.append(_dot(h[HALO:, :], wu_ref[...]))
    halo_rows = jnp.where(i % tiles_per_seq == 0, HALO, 0)
    rowid = lax.broadcasted_iota(jnp.int32, gates[0].shape, 0)
    gates[0] = jnp.where(rowid < halo_rows, 0.0, gates[0])
    for r0, gate_all, up in zip(subs, gates, ups):
        prev1 = pltpu.roll(gate_all, 1, 0)[HALO:, :]
        prev2 = pltpu.roll(gate_all, 2, 0)[HALO:, :]
        g = _ffn_conv(gate_all[HALO:, :], prev1, prev2, cw_ref, cb_ref)
        o_ref[r0:r0 + sub, :] += _dot((_silu(g) * up).astype(BF16), wd_ref[...])
    tail_ref[0] = gates[-1][HALO + sub - 8:, :]


def _ffn_prompt(x, weights, l, *, seq_len, tm, tf, sub=512):
    n, d = x.shape
    dff = weights[1].shape[-1]
    tiles_per_seq = seq_len // tm
    halo_blocks = tm // HALO
    return pl.pallas_call(
        functools.partial(_ffn_prompt_kernel, tm=tm, sub=sub, tiles_per_seq=tiles_per_seq),
        out_shape=[jax.ShapeDtypeStruct((n, d), F32),
                   jax.ShapeDtypeStruct((n // tm, 8, dff), F32)],
        grid=(n // tm, dff // tf),
        in_specs=[pl.BlockSpec((tm, d), lambda i, j: (i, 0)),
                  pl.BlockSpec((HALO, d), lambda i, j: (jnp.maximum(i * halo_blocks - 1, 0), 0)),
                  *_ffn_weight_specs(l, d, tf, lambda i, j: j)],
        out_specs=[pl.BlockSpec((tm, d), lambda i, j: (i, 0)),
                   pl.BlockSpec((1, 8, tf), lambda i, j: (i, 0, j))],
        scratch_shapes=[pltpu.VMEM((HALO + tm, d), BF16)],
        compiler_params=_params("parallel", "arbitrary"),
        name="ffn_prompt",
    )(x, x, *weights)


def _ffn_sample_kernel(x_ref, p1_ref, p2_ref, g_ref, wg_ref, wu_ref, cw_ref, cb_ref, wd_ref,
                       o_ref, gate_ref, h_ref, *, seq_len):
    @pl.when(pl.program_id(0) == 0)
    def _():
        x = x_ref[...]
        h_ref[...] = _rms(x, g_ref[...]).astype(BF16)
        o_ref[...] = x

    h = h_ref[...]
    gate = _dot(h, wg_ref[...])
    t = jnp.bitwise_and(lax.broadcasted_iota(jnp.int32, gate.shape, 0), seq_len - 1)
    prev1 = jnp.where(t >= 1, pltpu.roll(gate, 1, 0), p1_ref[...])
    prev2 = jnp.where(t >= 2, pltpu.roll(gate, 2, 0), p2_ref[...])
    g = _ffn_conv(gate, prev1, prev2, cw_ref, cb_ref)
    up = _dot(h, wu_ref[...])
    o_ref[...] += _dot((_silu(g) * up).astype(BF16), wd_ref[...])
    gate_ref[...] = gate


def _ffn_sample(x, p1, p2, weights, l, *, seq_len, tf):
    n, d = x.shape
    dff = weights[1].shape[-1]
    assert seq_len & (seq_len - 1) == 0
    return pl.pallas_call(
        functools.partial(_ffn_sample_kernel, seq_len=seq_len),
        out_shape=[jax.ShapeDtypeStruct((n, d), F32),
                   jax.ShapeDtypeStruct((n, dff), F32)],
        grid=(dff // tf,),
        in_specs=[pl.BlockSpec((n, d), lambda j: (0, 0)),
                  pl.BlockSpec((n, tf), lambda j: (0, j)),
                  pl.BlockSpec((n, tf), lambda j: (0, j)),
                  *_ffn_weight_specs(l, d, tf, lambda j: j)],
        out_specs=[pl.BlockSpec((n, d), lambda j: (0, 0)),
                   pl.BlockSpec((n, tf), lambda j: (0, j))],
        scratch_shapes=[pltpu.VMEM((n, d), BF16)],
        compiler_params=_params("arbitrary"),
        name="ffn_sample",
    )(x, p1, p2, *weights)


def _pack_w_in(w_in):
    nl, d, _ = w_in.shape
    o = 0
    seg = {}
    for name, width in (("cu", CM_W), ("cv", CM_W), ("gq", GLA_HEADS * GLA_DK), ("gk", GLA_HEADS * GLA_DK),
                        ("gv", GLA_V_W), ("gr", GLA_V_W), ("glr", GLA_RANK), ("sq", SB_W), ("sk", SB_W), ("sv", SB_W)):
        seg[name] = w_in[:, :, o:o + width]
        o += width

    def pad_heads(w):
        w = w.reshape(nl, d, GLA_HEADS, GLA_DK)
        return jnp.pad(w, ((0, 0), (0, 0), (0, 0), (0, LANES - GLA_DK))).reshape(nl, d, GLA_HEADS * LANES)

    w_all = jnp.concatenate([seg["cu"], seg["cv"], pad_heads(seg["gq"]), pad_heads(seg["gk"]),
                             seg["gv"], seg["gr"], seg["sq"], seg["sk"], seg["sv"]], axis=-1)
    w_glr = jnp.pad(seg["glr"], ((0, 0), (0, 0), (0, LANES - GLA_RANK)))
    return w_all.astype(BF16), w_glr.astype(BF16)


def _pack_gla_gate(gla_w2, gla_b):
    nl = gla_w2.shape[0]
    w2 = gla_w2.reshape(nl, GLA_RANK, GLA_HEADS, GLA_DK)
    w2 = jnp.pad(w2, ((0, 0), (0, LANES - GLA_RANK), (0, 0), (0, LANES - GLA_DK)))
    b = jnp.pad(gla_b.reshape(nl, 1, GLA_HEADS, GLA_DK), ((0, 0), (0, 0), (0, 0), (0, LANES - GLA_DK)))
    return w2.reshape(nl, LANES, GLA_HEADS * LANES).astype(BF16), b.reshape(nl, 1, GLA_HEADS * LANES)


def _state_to_kernel(s):
    st = jnp.swapaxes(s, 2, 3)
    return jnp.pad(st, ((0, 0), (0, 0), (0, 0), (0, LANES - GLA_DK)))


def _state_from_kernel(st):
    return jnp.swapaxes(st, 2, 3)[:, :, :GLA_DK, :]


def _later_sum_matrix():
    j = lax.broadcasted_iota(jnp.int32, (SB_BLK, 2 * SB_BLK), 0)
    s = lax.broadcasted_iota(jnp.int32, (SB_BLK, 2 * SB_BLK), 1)
    return jnp.where(jnp.logical_or(j > s, s >= SB_BLK), 1.0, 0.0).astype(BF16)


def _layer_prompt(x, lw, l, kv_bufs, *, nseq, seq_len, depth):
    n = x.shape[0]
    proj, k_all, v_all, glr = _in_proj(x, lw["norm1_g"], lw["w_all"], lw["w_glr"], lw["sb_k_g"], l, TM,
                                       kv_rows=depth * n, kv_row0=l * n, kv_bufs=kv_bufs)
    (oa,) = _chunk_mlp(proj, lw["cm_ws"][l], lw["cm_b"][l], chunk_mask=True, want_vn=False, tm=TM)
    s0t = jnp.zeros((nseq, GLA_HEADS, LANES, LANES), F32)
    ob, st = _gla(proj, glr, lw["gla_w2"], lw["gla_b"], lw["gla_gn"], s0t, l,
                  nseq=nseq, seq_len=seq_len, chunk=CHUNK, rows_per_step=TM)
    qtiles = seq_len // SB_BLK
    blk0 = l * n // SB_BLK

    def past_blk(s, i, w):
        return blk0 + s * qtiles + jnp.maximum(i - w, 0)

    oc = _stick_breaking(proj, lw["sb_q_g"], k_all, v_all, k_all, v_all, lw["uo"], l, nseq=nseq, tq=SB_BLK,
                         qtiles=qtiles, diag_blk0=blk0, past_blk=past_blk, n_past=None, max_past=qtiles - 1)
    x = _out_proj(x, oa, ob, oc, lw["w_out"], l, TM)
    x, tail = _ffn_prompt(x, lw["ffn"], l, seq_len=seq_len, tm=TM_FFN, tf=TF, sub=TM)
    tiles_per_seq = seq_len // TM_FFN
    conv_state = tail[tiles_per_seq - 1::tiles_per_seq, 8 - (CONV_W - 1):, :]
    return x, (k_all, v_all), _state_from_kernel(st), conv_state


def _layer_sample(x, lw, l, cache_k, cache_v, gla_s0, conv_prev, *, nseq, seq_len, past_len):
    n = x.shape[0]
    proj, k, v, glr = _in_proj(x, lw["norm1_g"], lw["w_all"], lw["w_glr"], lw["sb_k_g"], l, n,
                               kv_rows=n, kv_row0=0, kv_bufs=None)
    ws = lw["cm_ws"][l][:, :seq_len, :seq_len]
    ws_bd = jnp.einsum("ab,gij->gaibj", jnp.eye(nseq, dtype=F32), ws).reshape(CM_GROUPS, n, n)
    b_bd = jnp.tile(lw["cm_b"][l][:, :seq_len, :], (1, nseq, 1))
    oa, cm_v = _chunk_mlp(proj, ws_bd, b_bd, chunk_mask=False, want_vn=True, tm=n)
    ob, st = _gla(proj, glr, lw["gla_w2"], lw["gla_b"], lw["gla_gn"], _state_to_kernel(gla_s0), l,
                  nseq=nseq, seq_len=seq_len, chunk=seq_len, rows_per_step=seq_len)

    def pad_new(a):
        a = jnp.pad(a.reshape(nseq, seq_len, SB_HEADS, SB_DIM), ((0, 0), (0, SB_BLK - seq_len), (0, 0), (0, 0)))
        return a.reshape(nseq * SB_BLK * SB_HEADS, SB_DIM)

    past_blocks = past_len // SB_BLK

    def past_blk(s, i, w):
        return (l * nseq + s) * past_blocks + past_blocks - w

    oc = _stick_breaking(proj, lw["sb_q_g"], pad_new(k), pad_new(v), cache_k, cache_v, lw["uo"], l, nseq=nseq,
                         tq=seq_len, qtiles=1, diag_blk0=0, past_blk=past_blk, n_past=past_blocks,
                         max_past=past_blocks)
    x = _out_proj(x, oa, ob, oc, lw["w_out"], l, n)
    dff = conv_prev.shape[-1]
    p = jnp.zeros((nseq, seq_len, dff), F32)
    p1 = p.at[:, 0].set(conv_prev[:, 1]).reshape(n, dff)
    p2 = p.at[:, 0].set(conv_prev[:, 0]).at[:, 1].set(conv_prev[:, 1]).reshape(n, dff)
    x, gate = _ffn_sample(x, p1, p2, lw["ffn"], l, seq_len=seq_len, tf=TF)
    conv_state = gate.reshape(nseq, seq_len, dff)[:, seq_len - (CONV_W - 1):, :]
    return x, k, v, _state_from_kernel(st), conv_state, cm_v


def kernel(x_prompt, x_sample, cache_sb_k, cache_sb_v, state_gla, state_ffn_conv, norm1_g, w_in, cm_ws, cm_b,
           gla_w2, gla_b, gla_norm_g, sb_q_g, sb_k_g, w_out, norm2_g, ffn_w_gate, ffn_w_up, ffn_conv_w,
           ffn_conv_b, ffn_w_down):
    depth = w_in.shape[0]
    bp, tp, d = x_prompt.shape
    bs, ts, _ = x_sample.shape
    past_len = cache_sb_k.shape[2]
    assert d == D_MODEL and bs * ts == CM_BLOCK and tp % TM_FFN == 0 and past_len % SB_BLK == 0

    w_all, w_glr = _pack_w_in(w_in)
    w2, b2 = _pack_gla_gate(gla_w2, gla_b)
    lw = {
        "norm1_g": norm1_g[:, None, :], "w_all": w_all, "w_glr": w_glr,
        "cm_ws": cm_ws, "cm_b": cm_b[..., None],
        "gla_w2": w2, "gla_b": b2, "gla_gn": gla_norm_g.reshape(depth, 1, GLA_V_W),
        "sb_q_g": sb_q_g[:, None, :], "sb_k_g": sb_k_g[:, None, :], "uo": _later_sum_matrix(),
        "w_out": w_out.astype(BF16),
        "ffn": (norm2_g[:, None, :], ffn_w_gate.astype(BF16), ffn_w_up.astype(BF16), ffn_conv_w,
                ffn_conv_b[:, None, :], ffn_w_down.astype(BF16)),
    }
    cache_k = cache_sb_k.reshape(depth * bs * past_len * SB_HEADS, SB_DIM)
    cache_v = cache_sb_v.reshape(depth * bs * past_len * SB_HEADS, SB_DIM)

    yp = x_prompt.reshape(bp * tp, d)
    ys = x_sample.reshape(bs * ts, d)
    kv_p = None
    outs_p, outs_s = [], []
    for l in range(depth):
        yp, kv_p, sp, cp = _layer_prompt(yp, lw, l, kv_p, nseq=bp, seq_len=tp, depth=depth)
        ys, ks, vs, ss, cs, cmv = _layer_sample(ys, lw, l, cache_k, cache_v, state_gla[l], state_ffn_conv[l],
                                                nseq=bs, seq_len=ts, past_len=past_len)
        outs_p.append((sp, cp))
        outs_s.append((ks.reshape(bs, ts, SB_HEADS, SB_DIM), vs.reshape(bs, ts, SB_HEADS, SB_DIM), ss, cs,
                       cmv.reshape(bs, ts, CM_GROUPS, CM_DIM)))

    stack = lambda outs, i: jnp.stack([o[i] for o in outs])
    kv_shape = (depth, bp, tp, SB_HEADS, SB_DIM)
    return (yp.reshape(bp, tp, d), ys.reshape(bs, ts, d),
            kv_p[0].reshape(kv_shape), kv_p[1].reshape(kv_shape), stack(outs_p, 0), stack(outs_p, 1),
            stack(outs_s, 0), stack(outs_s, 1), stack(outs_s, 2), stack(outs_s, 3), stack(outs_s, 4))
```

```python
import functools

import jax
import jax.numpy as jnp
from jax import lax
from jax.experimental import pallas as pl
from jax.experimental.pallas import tpu as pltpu

F32 = jnp.float32
BF16 = jnp.bfloat16

D_MODEL = 2048
CHUNK = 64
CM_BLOCK = 128
CM_GROUPS = 4
CM_DIM = 128
CM_W = CM_GROUPS * CM_DIM
GLA_HEADS = 4
GLA_DK = 64
GLA_DV = 128
GLA_RANK = 16
GLA_TAU = 16.0
GLA_V_W = GLA_HEADS * GLA_DV
SB_HEADS = 8
SB_DIM = 128
SB_W = SB_HEADS * SB_DIM
D_FF = 5632
CONV_W = 3
EPS = 1e-6

LANES = 128
V7X_VMEM_LIMIT = 56 * 2 ** 20

TM = 512
TM_FFN = 1024
TN = 2048
TF = 512

COL_CM_U = 0
COL_CM_V = 512
COL_GQ = 1024
COL_GK = 1536
COL_GV = 2048
COL_GR = 2560
COL_SQ = 3072
PROJ_W = 4096
PROJ_TILES = PROJ_W // TN
KV_TILES = 1
assert 2 * SB_W == TN

SB_BLK = 128
SB_WINDOW = 2
SB_LAST_ROWS = SB_BLK
SB_LOG_ZERO = -105.0
SB_DONE = -1e30


def _dot(a, b):
    return jnp.dot(a, b, preferred_element_type=F32)


def _dot_nt(a, b):
    return lax.dot_general(a, b, (((1,), (1,)), ((), ())), preferred_element_type=F32)


def _dot_tn(a, b):
    return lax.dot_general(a, b, (((0,), (0,)), ((), ())), preferred_element_type=F32)


def _rms(x, g):
    ms = jnp.mean(x * x, axis=-1, keepdims=True)
    return x * lax.rsqrt(ms + EPS) * g


def _gelu_tanh(x):
    return x * (0.5 * (1.0 + jnp.tanh(0.7978845608028654 * (x + 0.044715 * (x * x * x)))))


def _log_sigmoid(x):
    return jnp.minimum(x, 0.0) - jnp.log(1.0 + jnp.exp(-jnp.abs(x)))


def _silu(x):
    return x * (1.0 / (1.0 + jnp.exp(-x)))


def _split_bf16(x):
    hi = x.astype(BF16)
    lo = (x - hi.astype(F32)).astype(BF16)
    return hi, lo


def _params(*sem):
    return pltpu.CompilerParams(dimension_semantics=sem, vmem_limit_bytes=V7X_VMEM_LIMIT)


def _head_cols(h):
    return slice(h * LANES, (h + 1) * LANES)


def _head_rows(h, ntok):
    return pl.ds(h, ntok, stride=SB_HEADS)


def _in_proj_kernel(*refs, n_alias):
    x_ref, g_ref, w_ref, wl_ref, gk_ref = refs[:5]
    proj_ref, k_ref, v_ref, glr_ref, h_ref = refs[5 + n_alias:]
    j = pl.program_id(1)

    @pl.when(j == 0)
    def _():
        h = _rms(x_ref[...], g_ref[...]).astype(BF16)
        h_ref[...] = h
        glr_ref[...] = _dot(h, wl_ref[...])

    @pl.when(j < PROJ_TILES)
    def _():
        proj_ref[...] = _dot(h_ref[...], w_ref[...]).astype(BF16)

    @pl.when(j == PROJ_TILES)
    def _():
        kv = _dot(h_ref[...], w_ref[...])
        rows = kv.shape[0]
        for h in range(SB_HEADS):
            k_ref[_head_rows(h, rows), :] = _rms(kv[:, _head_cols(h)], gk_ref[...])
        for h in range(SB_HEADS):
            v_ref[_head_rows(h, rows), :] = kv[:, _head_cols(SB_HEADS + h)]


def _in_proj(x, g, w_all, w_glr, gk, l, tm, *, kv_rows, kv_row0, kv_bufs):
    n, d = x.shape
    blk0 = kv_row0 // tm
    in_specs = [pl.BlockSpec((tm, d), lambda i, j: (i, 0)),
                pl.BlockSpec((None, 1, d), lambda i, j: (l, 0, 0)),
                pl.BlockSpec((None, d, TN), lambda i, j: (l, 0, j)),
                pl.BlockSpec((None, d, LANES), lambda i, j: (l, 0, 0)),
                pl.BlockSpec((None, 1, SB_DIM), lambda i, j: (l, 0, 0))]
    args = [x, g, w_all, w_glr, gk]
    aliases = {}
    if kv_bufs is not None:
        in_specs += [pl.BlockSpec(memory_space=pl.ANY)] * 2
        args += list(kv_bufs)
        aliases = {5: 1, 6: 2}
    kv_shape = jax.ShapeDtypeStruct((kv_rows * SB_HEADS, SB_DIM), F32)
    kv_spec = pl.BlockSpec((tm * SB_HEADS, SB_DIM), lambda i, j: (blk0 + i, 0))
    return pl.pallas_call(
        functools.partial(_in_proj_kernel, n_alias=len(aliases)),
        out_shape=[jax.ShapeDtypeStruct((n, PROJ_W), BF16), kv_shape, kv_shape,
                   jax.ShapeDtypeStruct((n, LANES), F32)],
        grid=(n // tm, PROJ_TILES + KV_TILES),
        in_specs=in_specs,
        out_specs=[pl.BlockSpec((tm, TN), lambda i, j: (i, jnp.minimum(j, PROJ_TILES - 1))),
                   kv_spec, kv_spec,
                   pl.BlockSpec((tm, LANES), lambda i, j: (i, 0))],
        scratch_shapes=[pltpu.VMEM((tm, d), BF16)],
        input_output_aliases=aliases,
        compiler_params=_params("parallel", "arbitrary"),
        name="in_proj",
    )(*args)


def _chunk_mlp_kernel(u_ref, v_ref, ws_ref, b_ref, o_ref, *vn_refs, chunk_mask, nblk):
    row = lax.broadcasted_iota(jnp.int32, (CM_BLOCK, CM_BLOCK), 0)
    col = lax.broadcasted_iota(jnp.int32, (CM_BLOCK, CM_BLOCK), 1)
    visible = jnp.logical_or(row >= CHUNK, col < CHUNK)
    for g in range(CM_GROUPS):
        w = ws_ref[g]
        if chunk_mask:
            w = jnp.where(visible, w, 0.0)
        w = w.astype(BF16)
        bias = b_ref[g]
        cols = slice(g * CM_DIM, (g + 1) * CM_DIM)
        for n in range(nblk):
            rows = slice(n * CM_BLOCK, (n + 1) * CM_BLOCK)
            v = _gelu_tanh(v_ref[rows, cols].astype(F32))
            d = v - jnp.mean(v, axis=-1, keepdims=True)
            vn = d * lax.rsqrt(jnp.mean(d * d, axis=-1, keepdims=True) + EPS)
            if vn_refs:
                vn_refs[0][rows, cols] = vn
            s = _dot(w, vn.astype(BF16)) + bias
            o_ref[rows, cols] = (_gelu_tanh(u_ref[rows, cols].astype(F32)) * s).astype(BF16)


def _chunk_mlp(proj, ws, b, *, chunk_mask, want_vn, tm):
    n = proj.shape[0]
    nblk = tm // CM_BLOCK
    out_shape = [jax.ShapeDtypeStruct((n, CM_W), BF16)]
    out_specs = [pl.BlockSpec((tm, CM_W), lambda i: (i, 0))]
    if want_vn:
        out_shape.append(jax.ShapeDtypeStruct((n, CM_W), F32))
        out_specs.append(pl.BlockSpec((tm, CM_W), lambda i: (i, 0)))
    return pl.pallas_call(
        functools.partial(_chunk_mlp_kernel, chunk_mask=chunk_mask, nblk=nblk),
        out_shape=out_shape,
        grid=(n // tm,),
        in_specs=[pl.BlockSpec((tm, CM_W), lambda i: (i, COL_CM_U // CM_W)),
                  pl.BlockSpec((tm, CM_W), lambda i: (i, COL_CM_V // CM_W)),
                  pl.BlockSpec((CM_GROUPS, CM_BLOCK, CM_BLOCK), lambda i: (0, 0, 0)),
                  pl.BlockSpec((CM_GROUPS, CM_BLOCK, 1), lambda i: (0, 0, 0))],
        out_specs=out_specs,
        compiler_params=_params("parallel"),
        name="chunk_mlp",
    )(proj, proj, ws, b)


def _gla_kernel(q_ref, k_ref, v_ref, r_ref, glr_ref, w2_ref, b_ref, gn_ref, s0_ref,
                o_ref, st_ref, s_scr, *, chunk, nchunk):
    c = pl.program_id(1)

    @pl.when(c == 0)
    def _():
        s_scr[...] = s0_ref[0]

    row = lax.broadcasted_iota(jnp.int32, (chunk, chunk), 0)
    col = lax.broadcasted_iota(jnp.int32, (chunk, chunk), 1)
    causal = col <= row
    tril = jnp.where(causal, 1.0, 0.0).astype(BF16)

    heads = range(GLA_HEADS)
    chunks = range(nchunk)
    x = _dot(glr_ref[...].astype(BF16), w2_ref[...]) + b_ref[...]
    hi, lo = _split_bf16(_log_sigmoid(x) * (1.0 / GLA_TAU))
    qes, kes, kds, vs, decays = {}, {}, {}, {}, {}
    for ci in chunks:
        rows = slice(ci * chunk, (ci + 1) * chunk)
        cum = _dot(tril, hi[rows, :]) + _dot(tril, lo[rows, :])
        last = cum[chunk - 1:chunk, :]
        for h in heads:
            cs = _head_cols(h)
            cum_h = cum[:, cs]
            k = k_ref[rows, cs].astype(F32)
            qes[ci, h] = ((q_ref[rows, cs].astype(F32) * (GLA_DK ** -0.5)) * jnp.exp(cum_h)).astype(BF16)
            kes[ci, h] = (k * jnp.exp(-cum_h)).astype(BF16)
            kds[ci, h] = (k * jnp.exp(last[:, cs] - cum_h)).astype(BF16)
            vs[ci, h] = v_ref[rows, cs].astype(BF16)
            decays[ci, h] = jnp.exp(last[:, cs])
    pairs = [(ci, h) for ci in chunks for h in heads]
    scores = {p: _dot_nt(qes[p], kes[p]) for p in pairs}
    updates = {p: _dot_tn(vs[p], kds[p]) for p in pairs}
    atts = {p: jnp.where(causal, scores[p], 0.0).astype(BF16) for p in pairs}
    states = [s_scr[h] for h in heads]
    for ci in chunks:
        rows = slice(ci * chunk, (ci + 1) * chunk)
        carried = [_dot_nt(qes[ci, h], states[h].astype(BF16)) for h in heads]
        local = [_dot(atts[ci, h], vs[ci, h]) for h in heads]
        for h in heads:
            cs = _head_cols(h)
            states[h] = states[h] * decays[ci, h] + updates[ci, h]
            on = _rms(local[h] + carried[h], gn_ref[:, cs])
            o_ref[rows, cs] = (on * _silu(r_ref[rows, cs].astype(F32))).astype(BF16)
    for h in heads:
        s_scr[h] = states[h]

    @pl.when(c == pl.num_programs(1) - 1)
    def _():
        st_ref[0] = s_scr[...]


def _gla(proj, glr, w2, b, gn, s0t, l, *, nseq, seq_len, chunk, rows_per_step):
    n = proj.shape[0]
    steps = seq_len // rows_per_step
    nchunk = rows_per_step // chunk
    w = GLA_V_W

    def rowmap(colblk):
        return lambda s, c: (s * steps + c, colblk)

    def layer(shape):
        return pl.BlockSpec((None,) + shape, lambda s, c: (l,) + (0,) * len(shape))

    return pl.pallas_call(
        functools.partial(_gla_kernel, chunk=chunk, nchunk=nchunk),
        out_shape=[jax.ShapeDtypeStruct((n, w), BF16),
                   jax.ShapeDtypeStruct((nseq, GLA_HEADS, LANES, LANES), F32)],
        grid=(nseq, steps),
        in_specs=[pl.BlockSpec((rows_per_step, w), rowmap(COL_GQ // w)),
                  pl.BlockSpec((rows_per_step, w), rowmap(COL_GK // w)),
                  pl.BlockSpec((rows_per_step, w), rowmap(COL_GV // w)),
                  pl.BlockSpec((rows_per_step, w), rowmap(COL_GR // w)),
                  pl.BlockSpec((rows_per_step, LANES), rowmap(0)),
                  layer((LANES, w)), layer((1, w)), layer((1, w)),
                  pl.BlockSpec((1, GLA_HEADS, LANES, LANES), lambda s, c: (s, 0, 0, 0))],
        out_specs=[pl.BlockSpec((rows_per_step, w), lambda s, c: (s * steps + c, 0)),
                   pl.BlockSpec((1, GLA_HEADS, LANES, LANES), lambda s, c: (s, 0, 0, 0))],
        scratch_shapes=[pltpu.VMEM((GLA_HEADS, LANES, LANES), F32)],
        compiler_params=_params("parallel", "arbitrary"),
        name="gla",
    )(proj, proj, proj, proj, glr, w2, b, gn, s0t)


def _sb_sweep(qns, blocks, carries, uo):
    heads = range(SB_HEADS)
    pairs = [(b, h) for b in range(len(blocks)) for h in heads]

    def top(x, b):
        rows = blocks[b][3]
        return x if rows is None else x[:rows]

    def put_top(x, b, new_top):
        rows = blocks[b][3]
        return new_top if rows is None else jnp.concatenate([new_top, x[rows:]], axis=0)

    zs = {(b, h): _dot_nt(top(qns[h], b), blocks[b][0][_head_rows(h, SB_BLK), :].astype(BF16)) * (SB_DIM ** -0.5)
          for b, h in pairs}
    log_betas, splits = {}, {}
    for b, h in pairs:
        mask = blocks[b][2]
        log_beta = _log_sigmoid(zs[b, h])
        log_stay = log_beta - zs[b, h]
        if mask is not None:
            log_stay = jnp.where(mask, log_stay, 0.0)
        log_betas[b, h] = log_beta
        splits[b, h] = _split_bf16(log_stay)
    sums = {p: _dot(splits[p][0], uo) + _dot(splits[p][1], uo) for p in pairs}
    carries = list(carries)
    weights = {}
    for b, h in pairs:
        mask = blocks[b][2]
        later = sums[b, h][:, :SB_BLK]
        total = sums[b, h][:, SB_BLK:]
        if carries[h] is not None:
            later = later + top(carries[h], b)
            total = put_top(carries[h], b, total + top(carries[h], b))
        carries[h] = total
        a = jnp.exp(log_betas[b, h] + later)
        if mask is not None:
            a = jnp.where(mask, a, 0.0)
        weights[b, h] = a.astype(BF16)
    outs = []
    for h in heads:
        out = None
        for b in range(len(blocks)):
            part = _dot(weights[b, h], blocks[b][1][_head_rows(h, SB_BLK), :].astype(BF16))
            out = part if out is None else put_top(out, b, top(out, b) + part)
        outs.append(out)
    return outs, carries


def _sb_queries(q_ref, gq_ref):
    return [_rms(q_ref[:, _head_cols(h)].astype(F32), gq_ref[...]).astype(BF16) for h in range(SB_HEADS)]


def _sb_diag_mask(tq):
    row = lax.broadcasted_iota(jnp.int32, (tq, SB_BLK), 0)
    col = lax.broadcasted_iota(jnp.int32, (tq, SB_BLK), 1)
    return col < row


def _sb_window_kernel(q_ref, gq_ref, kd_ref, vd_ref, k1_ref, v1_ref, k2_ref, v2_ref, uo_ref,
                      o_ref, mc_ref, *, tq, n_past):
    npast = pl.program_id(1) if n_past is None else n_past
    last_rows = min(tq, SB_LAST_ROWS)
    window = [(kd_ref, vd_ref, _sb_diag_mask(tq), None), (k1_ref, v1_ref, None, None),
              (k2_ref, v2_ref, None, None if last_rows == tq else last_rows)]

    def run(nblocks):
        outs, carries = _sb_sweep(_sb_queries(q_ref, gq_ref), window[:nblocks], [None] * SB_HEADS, uo_ref[...])
        done = jnp.where(npast > SB_WINDOW, 0.0, SB_DONE)
        for h in range(SB_HEADS):
            o_ref[:, _head_cols(h)] = outs[h].astype(BF16)
            if nblocks <= SB_WINDOW or last_rows == tq:
                worst = jnp.max(carries[h], axis=0, keepdims=True) + done
            else:
                worst = jnp.maximum(jnp.max(carries[h][:last_rows], axis=0, keepdims=True) + done,
                                    jnp.max(carries[h][last_rows:], axis=0, keepdims=True))
            mc_ref[0, h:h + 1, :] = worst

    if n_past is not None:
        run(1 + min(n_past, SB_WINDOW))
    else:
        for nb in range(1, SB_WINDOW + 1):
            pl.when(npast == nb - 1)(functools.partial(run, nb))
        pl.when(npast >= SB_WINDOW)(functools.partial(run, 1 + SB_WINDOW))


def _sb_window(proj, gq, kd, vd, kp, vp, uo, l, *, nseq, tq, qtiles, diag_blk0, past_blk, n_past):
    n = proj.shape[0]
    kv_blk = (SB_BLK * SB_HEADS, SB_DIM)
    past_specs = [pl.BlockSpec(kv_blk, functools.partial(lambda s, i, w: (past_blk(s, i, w), 0), w=w))
                  for w in (1, 1, 2, 2)]
    return pl.pallas_call(
        functools.partial(_sb_window_kernel, tq=tq, n_past=n_past),
        out_shape=[jax.ShapeDtypeStruct((n, SB_W), BF16),
                   jax.ShapeDtypeStruct((nseq * qtiles, SB_HEADS, LANES), F32)],
        grid=(nseq, qtiles),
        in_specs=[pl.BlockSpec((tq, SB_W), lambda s, i: (s * qtiles + i, COL_SQ // SB_W)),
                  pl.BlockSpec((None, 1, SB_DIM), lambda s, i: (l, 0, 0)),
                  pl.BlockSpec(kv_blk, lambda s, i: (diag_blk0 + s * qtiles + i, 0)),
                  pl.BlockSpec(kv_blk, lambda s, i: (diag_blk0 + s * qtiles + i, 0)),
                  *past_specs,
                  pl.BlockSpec((SB_BLK, 2 * SB_BLK), lambda s, i: (0, 0))],
        out_specs=[pl.BlockSpec((tq, SB_W), lambda s, i: (s * qtiles + i, 0)),
                   pl.BlockSpec((1, SB_HEADS, LANES), lambda s, i: (s * qtiles + i, 0, 0))],
        compiler_params=_params("parallel", "arbitrary"),
        name="sb_window",
    )(proj, gq, kd, vd, kp, vp, kp, vp, uo)


def _sb_full_kernel(flag_ref, q_ref, gq_ref, kd_ref, vd_ref, kp_ref, vp_ref, uo_ref, prev_ref,
                    o_ref, qn_scr, acc_scr, car_scr, live_scr, *, tq, qtiles, n_past):
    s, i, j = pl.program_id(0), pl.program_id(1), pl.program_id(2)
    npast = i if n_past is None else n_past
    flagged = flag_ref[s * qtiles + i] != 0

    heads = range(SB_HEADS)

    def keep(outs, carries, accumulate):
        for h in heads:
            cs = _head_cols(h)
            acc_scr[:, cs] = acc_scr[:, cs] + outs[h] if accumulate else outs[h]
            car_scr[:, cs] = carries[h]
        live_scr[0] = (jnp.max(car_scr[...]) > SB_LOG_ZERO).astype(jnp.int32)

    @pl.when(jnp.logical_and(flagged, j == 0))
    def _():
        qns = _sb_queries(q_ref, gq_ref)
        for h in heads:
            qn_scr[:, _head_cols(h)] = qns[h]
        keep(*_sb_sweep(qns, [(kd_ref, vd_ref, _sb_diag_mask(tq), None)], [None] * SB_HEADS, uo_ref[...]), False)

    @pl.when(jnp.logical_and(flagged, jnp.logical_and(j >= 1, j <= npast)))
    def _():
        @pl.when(live_scr[0] != 0)
        def _():
            qns = [qn_scr[:, _head_cols(h)] for h in heads]
            carries = [car_scr[:, _head_cols(h)] for h in heads]
            keep(*_sb_sweep(qns, [(kp_ref, vp_ref, None, None)], carries, uo_ref[...]), True)

    last = j == pl.num_programs(2) - 1

    @pl.when(jnp.logical_and(last, flagged))
    def _():
        o_ref[...] = acc_scr[...].astype(BF16)

    @pl.when(jnp.logical_and(last, jnp.logical_not(flagged)))
    def _():
        o_ref[...] = prev_ref[...]


def _sb_full(flags, proj, gq, kd, vd, kp, vp, uo, prev, l, *, nseq, tq, qtiles, diag_blk0, past_blk, n_past,
             max_past):
    n = proj.shape[0]
    kv_blk = (SB_BLK * SB_HEADS, SB_DIM)

    def past_map(s, i, j, flag_ref):
        w = jnp.clip(j, 1, max_past)
        return (jnp.where(flag_ref[s * qtiles + i] != 0, past_blk(s, i, w), past_blk(0, 0, 1)), 0)

    grid_spec = pltpu.PrefetchScalarGridSpec(
        num_scalar_prefetch=1,
        grid=(nseq, qtiles, max_past + 1),
        in_specs=[pl.BlockSpec((tq, SB_W), lambda s, i, j, f: (s * qtiles + i, COL_SQ // SB_W)),
                  pl.BlockSpec((None, 1, SB_DIM), lambda s, i, j, f: (l, 0, 0)),
                  pl.BlockSpec(kv_blk, lambda s, i, j, f: (diag_blk0 + s * qtiles + i, 0)),
                  pl.BlockSpec(kv_blk, lambda s, i, j, f: (diag_blk0 + s * qtiles + i, 0)),
                  pl.BlockSpec(kv_blk, past_map),
                  pl.BlockSpec(kv_blk, past_map),
                  pl.BlockSpec((SB_BLK, 2 * SB_BLK), lambda s, i, j, f: (0, 0)),
                  pl.BlockSpec((tq, SB_W), lambda s, i, j, f: (s * qtiles + i, 0))],
        out_specs=pl.BlockSpec((tq, SB_W), lambda s, i, j, f: (s * qtiles + i, 0)),
        scratch_shapes=[pltpu.VMEM((tq, SB_W), BF16), pltpu.VMEM((tq, SB_W), F32), pltpu.VMEM((tq, SB_W), F32),
                        pltpu.SMEM((1,), jnp.int32)])
    return pl.pallas_call(
        functools.partial(_sb_full_kernel, tq=tq, qtiles=qtiles, n_past=n_past),
        out_shape=jax.ShapeDtypeStruct((n, SB_W), BF16),
        grid_spec=grid_spec,
        compiler_params=_params("arbitrary", "arbitrary", "arbitrary"),
        name="sb_full",
    )(flags, proj, gq, kd, vd, kp, vp, uo, prev)


def _stick_breaking(proj, gq, kd, vd, kp, vp, uo, l, *, n_past, max_past, **geom):
    oc, carry = _sb_window(proj, gq, kd, vd, kp, vp, uo, l, n_past=n_past, **geom)
    flags = jnp.any(carry[:, :, 0] > SB_LOG_ZERO, axis=1).astype(jnp.int32)
    return lax.cond(jnp.any(flags != 0),
                    lambda: _sb_full(flags, proj, gq, kd, vd, kp, vp, uo, oc, l, n_past=n_past,
                                     max_past=max_past, **geom),
                    lambda: oc)


def _out_proj_kernel(x_ref, a_ref, b_ref, c_ref, w_ref, o_ref):
    o_ref[...] = (x_ref[...]
                  + _dot(a_ref[...], w_ref[0:CM_W, :])
                  + _dot(b_ref[...], w_ref[CM_W:CM_W + GLA_V_W, :])
                  + _dot(c_ref[...], w_ref[CM_W + GLA_V_W:, :]))


def _out_proj(x, oa, ob, oc, w, l, tm):
    n, d = x.shape
    return pl.pallas_call(
        _out_proj_kernel,
        out_shape=jax.ShapeDtypeStruct((n, d), F32),
        grid=(n // tm,),
        in_specs=[pl.BlockSpec((tm, d), lambda i: (i, 0)),
                  pl.BlockSpec((tm, CM_W), lambda i: (i, 0)),
                  pl.BlockSpec((tm, GLA_V_W), lambda i: (i, 0)),
                  pl.BlockSpec((tm, SB_W), lambda i: (i, 0)),
                  pl.BlockSpec((None,) + w.shape[1:], lambda i: (l, 0, 0))],
        out_specs=pl.BlockSpec((tm, d), lambda i: (i, 0)),
        compiler_params=_params("parallel"),
        name="out_proj",
    )(x, oa, ob, oc, w)


HALO = 16


def _ffn_conv(gate, prev1, prev2, cw_ref, cb_ref):
    return ((cb_ref[...] + cw_ref[0:1, :] * prev2) + cw_ref[1:2, :] * prev1) + cw_ref[2:3, :] * gate


def _ffn_weight_specs(l, d, tf, jmap):
    def spec(shape, blk):
        return pl.BlockSpec((None,) + shape, lambda *ids: (l,) + blk(jmap(*ids)))
    return [spec((1, d), lambda j: (0, 0)),
            spec((d, tf), lambda j: (0, j)),
            spec((d, tf), lambda j: (0, j)),
            spec((CONV_W, tf), lambda j: (0, j)),
            spec((1, tf), lambda j: (0, j)),
            spec((tf, d), lambda j: (j, 0))]


def _ffn_prompt_kernel(x_ref, xp_ref, g_ref, wg_ref, wu_ref, cw_ref, cb_ref, wd_ref,
                       o_ref, tail_ref, h_ref, *, tm, sub, tiles_per_seq):
    i = pl.program_id(0)
    subs = range(0, tm, sub)

    @pl.when(pl.program_id(1) == 0)
    def _():
        h_ref[0:HALO, :] = _rms(xp_ref[...], g_ref[...]).astype(BF16)
        for r0 in subs:
            x = x_ref[r0:r0 + sub, :]
            h_ref[HALO + r0:HALO + r0 + sub, :] = _rms(x, g_ref[...]).astype(BF16)
            o_ref[r0:r0 + sub, :] = x

    gates, ups = [], []
    for r0 in subs:
        h = h_ref[r0:r0 + HALO + sub, :]
        gates.append(_dot(h, wg_ref[...]))
        ups.append(_dot(h[HALO:, :], wu_ref[...]))
    halo_rows = jnp.where(i % tiles_per_seq == 0, HALO, 0)
    rowid = lax.broadcasted_iota(jnp.int32, gates[0].shape, 0)
    gates[0] = jnp.where(rowid < halo_rows, 0.0, gates[0])
    for r0, gate_all, up in zip(subs, gates, ups):
        prev1 = pltpu.roll(gate_all, 1, 0)[HALO:, :]
        prev2 = pltpu.roll(gate_all, 2, 0)[HALO:, :]
        g = _ffn_conv(gate_all[HALO:, :], prev1, prev2, cw_ref, cb_ref)
        o_ref[r0:r0 + sub, :] += _dot((_silu(g) * up).astype(BF16), wd_ref[...])
    tail_ref[0] = gates[-1][HALO + sub - 8:, :]


def _ffn_prompt(x, weights, l, *, seq_len, tm, tf, sub=512):
    n, d = x.shape
    dff = weights[1].shape[-1]
    tiles_per_seq = seq_len // tm
    halo_blocks = tm // HALO
    return pl.pallas_call(
        functools.partial(_ffn_prompt_kernel, tm=tm, sub=sub, tiles_per_seq=tiles_per_seq),
        out_shape=[jax.ShapeDtypeStruct((n, d), F32),
                   jax.ShapeDtypeStruct((n // tm, 8, dff), F32)],
        grid=(n // tm, dff // tf),
        in_specs=[pl.BlockSpec((tm, d), lambda i, j: (i, 0)),
                  pl.BlockSpec((HALO, d), lambda i, j: (jnp.maximum(i * halo_blocks - 1, 0), 0)),
                  *_ffn_weight_specs(l, d, tf, lambda i, j: j)],
        out_specs=[pl.BlockSpec((tm, d), lambda i, j: (i, 0)),
                   pl.BlockSpec((1, 8, tf), lambda i, j: (i, 0, j))],
        scratch_shapes=[pltpu.VMEM((HALO + tm, d), BF16)],
        compiler_params=_params("parallel", "arbitrary"),
        name="ffn_prompt",
    )(x, x, *weights)


def _ffn_sample_kernel(x_ref, p1_ref, p2_ref, g_ref, wg_ref, wu_ref, cw_ref, cb_ref, wd_ref,
                       o_ref, gate_ref, h_ref, *, seq_len):
    @pl.when(pl.program_id(0) == 0)
    def _():
        x = x_ref[...]
        h_ref[...] = _rms(x, g_ref[...]).astype(BF16)
        o_ref[...] = x

    h = h_ref[...]
    gate = _dot(h, wg_ref[...])
    t = jnp.bitwise_and(lax.broadcasted_iota(jnp.int32, gate.shape, 0), seq_len - 1)
    prev1 = jnp.where(t >= 1, pltpu.roll(gate, 1, 0), p1_ref[...])
    prev2 = jnp.where(t >= 2, pltpu.roll(gate, 2, 0), p2_ref[...])
    g = _ffn_conv(gate, prev1, prev2, cw_ref, cb_ref)
    up = _dot(h, wu_ref[...])
    o_ref[...] += _dot((_silu(g) * up).astype(BF16), wd_ref[...])
    gate_ref[...] = gate


def _ffn_sample(x, p1, p2, weights, l, *, seq_len, tf):
    n, d = x.shape
    dff = weights[1].shape[-1]
    assert seq_len & (seq_len - 1) == 0
    return pl.pallas_call(
        functools.partial(_ffn_sample_kernel, seq_len=seq_len),
        out_shape=[jax.ShapeDtypeStruct((n, d), F32),
                   jax.ShapeDtypeStruct((n, dff), F32)],
        grid=(dff // tf,),
        in_specs=[pl.BlockSpec((n, d), lambda j: (0, 0)),
                  pl.BlockSpec((n, tf), lambda j: (0, j)),
                  pl.BlockSpec((n, tf), lambda j: (0, j)),
                  *_ffn_weight_specs(l, d, tf, lambda j: j)],
        out_specs=[pl.BlockSpec((n, d), lambda j: (0, 0)),
                   pl.BlockSpec((n, tf), lambda j: (0, j))],
        scratch_shapes=[pltpu.VMEM((n, d), BF16)],
        compiler_params=_params("arbitrary"),
        name="ffn_sample",
    )(x, p1, p2, *weights)


def _pack_w_in(w_in):
    nl, d, _ = w_in.shape
    o = 0
    seg = {}
    for name, width in (("cu", CM_W), ("cv", CM_W), ("gq", GLA_HEADS * GLA_DK), ("gk", GLA_HEADS * GLA_DK),
                        ("gv", GLA_V_W), ("gr", GLA_V_W), ("glr", GLA_RANK), ("sq", SB_W), ("sk", SB_W), ("sv", SB_W)):
        seg[name] = w_in[:, :, o:o + width]
        o += width

    def pad_heads(w):
        w = w.reshape(nl, d, GLA_HEADS, GLA_DK)
        return jnp.pad(w, ((0, 0), (0, 0), (0, 0), (0, LANES - GLA_DK))).reshape(nl, d, GLA_HEADS * LANES)

    w_all = jnp.concatenate([seg["cu"], seg["cv"], pad_heads(seg["gq"]), pad_heads(seg["gk"]),
                             seg["gv"], seg["gr"], seg["sq"], seg["sk"], seg["sv"]], axis=-1)
    w_glr = jnp.pad(seg["glr"], ((0, 0), (0, 0), (0, LANES - GLA_RANK)))
    return w_all.astype(BF16), w_glr.astype(BF16)


def _pack_gla_gate(gla_w2, gla_b):
    nl = gla_w2.shape[0]
    w2 = gla_w2.reshape(nl, GLA_RANK, GLA_HEADS, GLA_DK)
    w2 = jnp.pad(w2, ((0, 0), (0, LANES - GLA_RANK), (0, 0), (0, LANES - GLA_DK)))
    b = jnp.pad(gla_b.reshape(nl, 1, GLA_HEADS, GLA_DK), ((0, 0), (0, 0), (0, 0), (0, LANES - GLA_DK)))
    return w2.reshape(nl, LANES, GLA_HEADS * LANES).astype(BF16), b.reshape(nl, 1, GLA_HEADS * LANES)


def _state_to_kernel(s):
    st = jnp.swapaxes(s, 2, 3)
    return jnp.pad(st, ((0, 0), (0, 0), (0, 0), (0, LANES - GLA_DK)))


def _state_from_kernel(st):
    return jnp.swapaxes(st, 2, 3)[:, :, :GLA_DK, :]


def _later_sum_matrix():
    j = lax.broadcasted_iota(jnp.int32, (SB_BLK, 2 * SB_BLK), 0)
    s = lax.broadcasted_iota(jnp.int32, (SB_BLK, 2 * SB_BLK), 1)
    return jnp.where(jnp.logical_or(j > s, s >= SB_BLK), 1.0, 0.0).astype(BF16)


def _layer_prompt(x, lw, l, kv_bufs, *, nseq, seq_len, depth):
    n = x.shape[0]
    proj, k_all, v_all, glr = _in_proj(x, lw["norm1_g"], lw["w_all"], lw["w_glr"], lw["sb_k_g"], l, TM,
                                       kv_rows=depth * n, kv_row0=l * n, kv_bufs=kv_bufs)
    (oa,) = _chunk_mlp(proj, lw["cm_ws"][l], lw["cm_b"][l], chunk_mask=True, want_vn=False, tm=TM)
    s0t = jnp.zeros((nseq, GLA_HEADS, LANES, LANES), F32)
    ob, st = _gla(proj, glr, lw["gla_w2"], lw["gla_b"], lw["gla_gn"], s0t, l,
                  nseq=nseq, seq_len=seq_len, chunk=CHUNK, rows_per_step=TM)
    qtiles = seq_len // SB_BLK
    blk0 = l * n // SB_BLK

    def past_blk(s, i, w):
        return blk0 + s * qtiles + jnp.maximum(i - w, 0)

    oc = _stick_breaking(proj, lw["sb_q_g"], k_all, v_all, k_all, v_all, lw["uo"], l, nseq=nseq, tq=SB_BLK,
                         qtiles=qtiles, diag_blk0=blk0, past_blk=past_blk, n_past=None, max_past=qtiles - 1)
    x = _out_proj(x, oa, ob, oc, lw["w_out"], l, TM)
    x, tail = _ffn_prompt(x, lw["ffn"], l, seq_len=seq_len, tm=TM_FFN, tf=TF, sub=TM)
    tiles_per_seq = seq_len // TM_FFN
    conv_state = tail[tiles_per_seq - 1::tiles_per_seq, 8 - (CONV_W - 1):, :]
    return x, (k_all, v_all), _state_from_kernel(st), conv_state


def _layer_sample(x, lw, l, cache_k, cache_v, gla_s0, conv_prev, *, nseq, seq_len, past_len):
    n = x.shape[0]
    proj, k, v, glr = _in_proj(x, lw["norm1_g"], lw["w_all"], lw["w_glr"], lw["sb_k_g"], l, n,
                               kv_rows=n, kv_row0=0, kv_bufs=None)
    ws = lw["cm_ws"][l][:, :seq_len, :seq_len]
    ws_bd = jnp.einsum("ab,gij->gaibj", jnp.eye(nseq, dtype=F32), ws).reshape(CM_GROUPS, n, n)
    b_bd = jnp.tile(lw["cm_b"][l][:, :seq_len, :], (1, nseq, 1))
    oa, cm_v = _chunk_mlp(proj, ws_bd, b_bd, chunk_mask=False, want_vn=True, tm=n)
    ob, st = _gla(proj, glr, lw["gla_w2"], lw["gla_b"], lw["gla_gn"], _state_to_kernel(gla_s0), l,
                  nseq=nseq, seq_len=seq_len, chunk=seq_len, rows_per_step=seq_len)

    def pad_new(a):
        a = jnp.pad(a.reshape(nseq, seq_len, SB_HEADS, SB_DIM), ((0, 0), (0, SB_BLK - seq_len), (0, 0), (0, 0)))
        return a.reshape(nseq * SB_BLK * SB_HEADS, SB_DIM)

    past_blocks = past_len // SB_BLK

    def past_blk(s, i, w):
        return (l * nseq + s) * past_blocks + past_blocks - w

    oc = _stick_breaking(proj, lw["sb_q_g"], pad_new(k), pad_new(v), cache_k, cache_v, lw["uo"], l, nseq=nseq,
                         tq=seq_len, qtiles=1, diag_blk0=0, past_blk=past_blk, n_past=past_blocks,
                         max_past=past_blocks)
    x = _out_proj(x, oa, ob, oc, lw["w_out"], l, n)
    dff = conv_prev.shape[-1]
    p = jnp.zeros((nseq, seq_len, dff), F32)
    p1 = p.at[:, 0].set(conv_prev[:, 1]).reshape(n, dff)
    p2 = p.at[:, 0].set(conv_prev[:, 0]).at[:, 1].set(conv_prev[:, 1]).reshape(n, dff)
    x, gate = _ffn_sample(x, p1, p2, lw["ffn"], l, seq_len=seq_len, tf=TF)
    conv_state = gate.reshape(nseq, seq_len, dff)[:, seq_len - (CONV_W - 1):, :]
    return x, k, v, _state_from_kernel(st), conv_state, cm_v


def kernel(x_prompt, x_sample, cache_sb_k, cache_sb_v, state_gla, state_ffn_conv, norm1_g, w_in, cm_ws, cm_b,
           gla_w2, gla_b, gla_norm_g, sb_q_g, sb_k_g, w_out, norm2_g, ffn_w_gate, ffn_w_up, ffn_conv_w,
           ffn_conv_b, ffn_w_down):
    depth = w_in.shape[0]
    bp, tp, d = x_prompt.shape
    bs, ts, _ = x_sample.shape
    past_len = cache_sb_k.shape[2]
    assert d == D_MODEL and bs * ts == CM_BLOCK and tp % TM_FFN == 0 and past_len % SB_BLK == 0

    w_all, w_glr = _pack_w_in(w_in)
    w2, b2 = _pack_gla_gate(gla_w2, gla_b)
    lw = {
        "norm1_g": norm1_g[:, None, :], "w_all": w_all, "w_glr": w_glr,
        "cm_ws": cm_ws, "cm_b": cm_b[..., None],
        "gla_w2": w2, "gla_b": b2, "gla_gn": gla_norm_g.reshape(depth, 1, GLA_V_W),
        "sb_q_g": sb_q_g[:, None, :], "sb_k_g": sb_k_g[:, None, :], "uo": _later_sum_matrix(),
        "w_out": w_out.astype(BF16),
        "ffn": (norm2_g[:, None, :], ffn_w_gate.astype(BF16), ffn_w_up.astype(BF16), ffn_conv_w,
                ffn_conv_b[:, None, :], ffn_w_down.astype(BF16)),
    }
    cache_k = cache_sb_k.reshape(depth * bs * past_len * SB_HEADS, SB_DIM)
    cache_v = cache_sb_v.reshape(depth * bs * past_len * SB_HEADS, SB_DIM)

    yp = x_prompt.reshape(bp * tp, d)
    ys = x_sample.reshape(bs * ts, d)
    kv_p = None
    outs_p, outs_s = [], []
    for l in range(depth):
        yp, kv_p, sp, cp = _layer_prompt(yp, lw, l, kv_p, nseq=bp, seq_len=tp, depth=depth)
        ys, ks, vs, ss, cs, cmv = _layer_sample(ys, lw, l, cache_k, cache_v, state_gla[l], state_ffn_conv[l],
                                                nseq=bs, seq_len=ts, past_len=past_len)
        outs_p.append((sp, cp))
        outs_s.append((ks.reshape(bs, ts, SB_HEADS, SB_DIM), vs.reshape(bs, ts, SB_HEADS, SB_DIM), ss, cs,
                       cmv.reshape(bs, ts, CM_GROUPS, CM_DIM)))

    stack = lambda outs, i: jnp.stack([o[i] for o in outs])
    kv_shape = (depth, bp, tp, SB_HEADS, SB_DIM)
    return (yp.reshape(bp, tp, d), ys.reshape(bs, ts, d),
            kv_p[0].reshape(kv_shape), kv_p[1].reshape(kv_shape), stack(outs_p, 0), stack(outs_p, 1),
            stack(outs_s, 0), stack(outs_s, 1), stack(outs_s, 2), stack(outs_s, 3), stack(outs_s, 4))
```
